```python
import jax, jax.numpy as jnp
from jax import lax
import numpy as np

D_MODEL = 1024
BATCH = 8
SEQ = 4096
DEPTH = 1

N_META = 16
CONV_CH = 512
CONV_KERNEL = 31
RWKV_WIDTH = 512
RWKV_HEAD = 64
RWKV_HEADS = RWKV_WIDTH // RWKV_HEAD
RANK_W = 64
RANK_A = 64
RANK_G = 128
N_BRANCH = 2
SHIFT_TOTAL = 3 * RWKV_WIDTH + RANK_W + RANK_A + RANK_G
IN_SPLITS = [CONV_CH, 2 * CONV_CH, 2 * CONV_CH + SHIFT_TOTAL]
IN_TOTAL = 2 * CONV_CH + SHIFT_TOTAL + N_BRANCH * D_MODEL
RWKV_SPLITS = [RWKV_WIDTH, 2 * RWKV_WIDTH, 3 * RWKV_WIDTH, 3 * RWKV_WIDTH + RANK_W, 3 * RWKV_WIDTH + RANK_W + RANK_A]
PEER_HEADS = 8
PEER_NKEYS = 128
PEER_EXPERTS = PEER_NKEYS * PEER_NKEYS
PEER_DHALF = 128
PEER_TOPK = 16
TOK_BLOCK = 256
RMS_EPS = 1e-6
LN_EPS = 1e-5
GN_EPS = 64e-5

kernel_name = "hybrid_conformer_rwkv7_peer_block"


def rmsnorm(x, g):
    xf = x.astype(jnp.float32)
    y = xf * lax.rsqrt(jnp.mean(xf * xf, axis=-1, keepdims=True) + RMS_EPS)
    return (y * g.astype(jnp.float32)).astype(x.dtype)


def token_shift(z):
    return jnp.pad(z[:, :-1], ((0, 0), (1, 0), (0, 0)))


def conformer_conv(val, gate, conv_w, conv_b, ln_g, ln_b, w_out):
    u = val * jax.nn.sigmoid(gate)
    y = lax.conv_general_dilated(
        u, conv_w[:, None, :].astype(u.dtype), window_strides=(1,),
        padding=[(CONV_KERNEL - 1, 0)], dimension_numbers=('NWC', 'WIO', 'NWC'),
        feature_group_count=CONV_CH) + conv_b
    yf = y.astype(jnp.float32)
    mean = jnp.mean(yf, axis=-1, keepdims=True)
    var = jnp.mean(jnp.square(yf - mean), axis=-1, keepdims=True)
    yn = (yf - mean) * lax.rsqrt(var + LN_EPS) * ln_g + ln_b
    return jax.nn.silu(yn).astype(val.dtype) @ w_out


def wkv7_scan(r, decay, k, v, kk, a):
    def step(S, inp):
        r_t, w_t, k_t, v_t, kk_t, a_t = inp
        sa = jnp.einsum('bhvk,bhk->bhv', S, kk_t)
        S = (S * w_t[:, :, None, :]
             - sa[..., None] * (kk_t * a_t)[:, :, None, :]
             + v_t[..., None] * k_t[:, :, None, :])
        return S, jnp.einsum('bhvk,bhk->bhv', S, r_t)
    B, T, H, N = r.shape
    S0 = jnp.zeros((B, H, N, N), jnp.float32)
    xs = tuple(jnp.moveaxis(z, 1, 0) for z in (r, decay, k, v, kk, a))
    _, o = lax.scan(step, S0, xs)
    return jnp.moveaxis(o, 0, 1)


def rwkv7_mix(z, mu, w0, w_up, a0, a_up, g_up, k_k, k_a, r_k, gn_g, gn_b, w_out):
    B, T, _ = z.shape
    zf = z.astype(jnp.float32)
    zf = zf + (token_shift(zf) - zf) * mu
    r, k, v, lw, la, lg = jnp.split(zf, RWKV_SPLITS, axis=-1)
    w = -jax.nn.softplus(-(w0 + jnp.tanh(lw) @ w_up)) - 0.5
    decay = jnp.exp(-jnp.exp(w))
    a = jax.nn.sigmoid(a0 + la @ a_up)
    g = jax.nn.sigmoid(lg) @ g_up
    hd = lambda t: t.reshape(B, T, RWKV_HEADS, RWKV_HEAD)
    kk = hd(k * k_k)
    kk = kk * lax.rsqrt(jnp.maximum(jnp.sum(kk * kk, axis=-1, keepdims=True), 1e-24))
    k = k * (1.0 + (a - 1.0) * k_a)
    r_h, k_h, v_h, a_h = hd(r), hd(k), hd(v), hd(a)
    o = wkv7_scan(r_h, hd(decay), k_h, v_h, kk, a_h)
    mean = jnp.mean(o, axis=-1, keepdims=True)
    var = jnp.mean(jnp.square(o - mean), axis=-1, keepdims=True)
    o = ((o - mean) * lax.rsqrt(var + GN_EPS)).reshape(B, T, RWKV_WIDTH) * gn_g + gn_b
    bonus = jnp.sum(r_h * k_h * r_k, axis=-1, keepdims=True) * v_h
    o = (o + bonus.reshape(B, T, RWKV_WIDTH)) * g
    return o.astype(z.dtype) @ w_out


def peer_ffn(x, w_q, sub_keys, expert_u, expert_v):
    B, T, D = x.shape
    n_tok = B * T
    n_blocks = -(-n_tok // TOK_BLOCK)
    pad = n_blocks * TOK_BLOCK - n_tok
    xt = jnp.pad(x.reshape(n_tok, D), ((0, pad), (0, 0))).reshape(n_blocks, TOK_BLOCK, D)
    K = PEER_TOPK

    def block(xb):
        q = (xb @ w_q).reshape(TOK_BLOCK, PEER_HEADS, 2, PEER_DHALF)
        s = jnp.einsum('thcd,hcnd->thcn', q, sub_keys).astype(jnp.float32)
        top_s, top_i = lax.top_k(s, K)
        cand = top_s[:, :, 0, :, None] + top_s[:, :, 1, None, :]
        best_s, best_c = lax.top_k(cand.reshape(TOK_BLOCK, PEER_HEADS, K * K), K)
        i1 = jnp.take_along_axis(top_i[:, :, 0], best_c // K, axis=-1)
        i2 = jnp.take_along_axis(top_i[:, :, 1], best_c % K, axis=-1)
        e = (i1 * PEER_NKEYS + i2).reshape(TOK_BLOCK, PEER_HEADS * K)
        gate = jax.nn.softmax(best_s, axis=-1).reshape(TOK_BLOCK, PEER_HEADS * K)
        u = expert_u[e]
        act = jax.nn.gelu(jnp.einsum('td,ted->te', xb, u), approximate=False)
        coef = (gate * act.astype(jnp.float32)).astype(xb.dtype)
        return jnp.einsum('te,ted->td', coef, expert_v[e])

    y = lax.map(block, xt)
    return y.reshape(n_blocks * TOK_BLOCK, D)[:n_tok].reshape(B, T, D)


def setup_inputs(seed: int = 0) -> dict:
    key = jax.random.key(seed)
    ks = jax.random.split(key, 32)
    f32 = jnp.float32
    L = DEPTH
    D = D_MODEL

    def nrm(k, shape, s):
        return jax.random.normal(k, shape, f32) * s

    return {
        "x": nrm(ks[0], (BATCH, SEQ, D), 1.0),
        "meta_tokens": nrm(ks[1], (N_META, D), 1.0),
        "g_mix": 1.0 + nrm(ks[2], (L, D), 0.01),
        "w_in": nrm(ks[3], (L, D, IN_TOTAL), D ** -0.5),
        "conv_w": nrm(ks[4], (L, CONV_KERNEL, CONV_CH), CONV_KERNEL ** -0.5),
        "conv_b": nrm(ks[5], (L, CONV_CH), 0.01),
        "conv_ln_g": 1.0 + nrm(ks[6], (L, CONV_CH), 0.01),
        "conv_ln_b": nrm(ks[7], (L, CONV_CH), 0.01),
        "w_conv_out": nrm(ks[8], (L, CONV_CH, D), CONV_CH ** -0.5),
        "mu_shift": jax.random.uniform(ks[9], (L, SHIFT_TOTAL), f32),
        "w0": jax.random.uniform(ks[10], (L, RWKV_WIDTH), f32, -6.0, -1.0),
        "w_up": nrm(ks[11], (L, RANK_W, RWKV_WIDTH), 0.1 * RANK_W ** -0.5),
        "a0": nrm(ks[12], (L, RWKV_WIDTH), 0.1),
        "a_up": nrm(ks[13], (L, RANK_A, RWKV_WIDTH), 0.1 * RANK_A ** -0.5),
        "g_up": nrm(ks[14], (L, RANK_G, RWKV_WIDTH), RANK_G ** -0.5),
        "k_k": 0.85 + nrm(ks[15], (L, RWKV_WIDTH), 0.02),
        "k_a": 1.0 + nrm(ks[16], (L, RWKV_WIDTH), 0.02),
        "r_k": nrm(ks[17], (L, RWKV_HEADS, RWKV_HEAD), 0.1),
        "gn_g": 1.0 + nrm(ks[18], (L, RWKV_WIDTH), 0.01),
        "gn_b": nrm(ks[19], (L, RWKV_WIDTH), 0.01),
        "w_rwkv_out": nrm(ks[20], (L, RWKV_WIDTH, D), RWKV_WIDTH ** -0.5),
        "w_o": nrm(ks[21], (L, D, D), D ** -0.5),
        "g_ffn": 1.0 + nrm(ks[22], (L, D), 0.01),
        "w_q": nrm(ks[23], (L, D, PEER_HEADS * 2 * PEER_DHALF), D ** -0.5),
        "sub_keys": nrm(ks[24], (L, PEER_HEADS, 2, PEER_NKEYS, PEER_DHALF), PEER_DHALF ** -0.5),
        "expert_u": nrm(ks[25], (L, PEER_EXPERTS, D), D ** -0.5),
        "expert_v": nrm(ks[26], (L, PEER_EXPERTS, D), PEER_HEADS ** -0.5),
        "g_final": 1.0 + nrm(ks[27], (D,), 0.01),
    }


def reference(x, meta_tokens, g_mix, w_in, conv_w, conv_b, conv_ln_g, conv_ln_b, w_conv_out,
              mu_shift, w0, w_up, a0, a_up, g_up, k_k, k_a, r_k, gn_g, gn_b, w_rwkv_out,
              w_o, g_ffn, w_q, sub_keys, expert_u, expert_v, g_final):
    B = x.shape[0]
    meta = jnp.broadcast_to(meta_tokens[None].astype(x.dtype), (B, N_META, D_MODEL))
    h = jnp.concatenate([meta, x], axis=1)
    for l in range(DEPTH):
        z = rmsnorm(h, g_mix[l]) @ w_in[l]
        conv_val, conv_gate, rwkv_in, gate_logits = jnp.split(z, IN_SPLITS, axis=-1)
        y_a = conformer_conv(conv_val, conv_gate, conv_w[l], conv_b[l], conv_ln_g[l], conv_ln_b[l], w_conv_out[l])
        y_b = rwkv7_mix(rwkv_in, mu_shift[l], w0[l], w_up[l], a0[l], a_up[l], g_up[l],
                        k_k[l], k_a[l], r_k[l], gn_g[l], gn_b[l], w_rwkv_out[l])
        gate_a, gate_b = jnp.split(jax.nn.sigmoid(gate_logits), N_BRANCH, axis=-1)
        h = h + (gate_a * y_a + gate_b * y_b) @ w_o[l]
        h = h + peer_ffn(rmsnorm(h, g_ffn[l]), w_q[l], sub_keys[l], expert_u[l], expert_v[l])
    return rmsnorm(h, g_final)[:, N_META:]
```

```python
import functools
import math

import jax
import jax.numpy as jnp
from jax import lax
from jax.experimental import pallas as pl
from jax.experimental.pallas import tpu as pltpu

F32 = jnp.float32
BF16 = jnp.bfloat16

CONV_CH = 512
CONV_KERNEL = 31
RWKV_WIDTH = 512
RWKV_HEAD = 64
RWKV_HEADS = RWKV_WIDTH // RWKV_HEAD
RANK_W = 64
RANK_A = 64
RANK_G = 128
SHIFT_TOTAL = 3 * RWKV_WIDTH + RANK_W + RANK_A + RANK_G
PEER_HEADS = 8
PEER_NKEYS = 128
PEER_DHALF = 128
PEER_TOPK = 16
RMS_EPS = 1e-6
LN_EPS = 1e-5
GN_EPS = 64e-5

LANES = 128
SUBLANES = 8
VMEM_LIMIT_BYTES = 56 * 1024 * 1024

CHUNK = 64
HALO = 32
GROUP_LANES = 256
PAIR_LANES = 128

NT_DIMS = (((1,), (1,)), ((), ()))
TN_DIMS = (((0,), (0,)), ((), ()))


def _sigmoid(x):
    return 1.0 / (1.0 + jnp.exp(-x))


def _dot(a, b):
    return jnp.dot(a.astype(BF16), b.astype(BF16), preferred_element_type=F32)


def _dot_nt(a, b):
    return lax.dot_general(a.astype(BF16), b.astype(BF16), NT_DIMS, preferred_element_type=F32)


def _dot_tn(a, b):
    return lax.dot_general(a.astype(BF16), b.astype(BF16), TN_DIMS, preferred_element_type=F32)


def _split3(x):
    hi = x.astype(BF16)
    r1 = x - hi.astype(F32)
    mid = r1.astype(BF16)
    lo = (r1 - mid.astype(F32)).astype(BF16)
    return hi, mid, lo


def _dot_x3(x, w_bf16):
    hi, mid, lo = _split3(x)
    acc = jnp.dot(hi, w_bf16, preferred_element_type=F32)
    acc = acc + jnp.dot(mid, w_bf16, preferred_element_type=F32)
    return acc + jnp.dot(lo, w_bf16, preferred_element_type=F32)


def _dot_w3(w_bf16, x):
    hi, mid, lo = _split3(x)
    acc = jnp.dot(w_bf16, hi, preferred_element_type=F32)
    acc = acc + jnp.dot(w_bf16, mid, preferred_element_type=F32)
    return acc + jnp.dot(w_bf16, lo, preferred_element_type=F32)


def _rmsnorm(x, g):
    ms = jnp.mean(x * x, axis=-1, keepdims=True)
    return x * lax.rsqrt(ms + RMS_EPS) * g


def _params(semantics):
    return pltpu.CompilerParams(dimension_semantics=semantics, vmem_limit_bytes=VMEM_LIMIT_BYTES)


def _full(shape):
    nd = len(shape)
    return pl.BlockSpec(shape, lambda *_: (0,) * nd)


def _inproj_body(x_ref, g_ref, wc_ref, wr_ref, wg_ref, u_ref, zr_ref, gate_ref):
    xn = _rmsnorm(x_ref[...], g_ref[...]).astype(BF16)
    zc = jnp.dot(xn, wc_ref[...], preferred_element_type=F32)
    u_ref[...] = zc[:, :CONV_CH] * _sigmoid(zc[:, CONV_CH:])
    zr_ref[...] = jnp.dot(xn, wr_ref[...], preferred_element_type=F32)
    gate_ref[...] = _sigmoid(jnp.dot(xn, wg_ref[...], preferred_element_type=F32))


def _inproj(x2d, g, wc, wr, wg, tm):
    n, d = x2d.shape
    ng = wg.shape[1]
    return pl.pallas_call(
        _inproj_body,
        grid=(n // tm,),
        in_specs=[
            pl.BlockSpec((tm, d), lambda i: (i, 0)),
            _full(g.shape), _full(wc.shape), _full(wr.shape), _full(wg.shape),
        ],
        out_specs=[
            pl.BlockSpec((tm, CONV_CH), lambda i: (i, 0)),
            pl.BlockSpec((tm, SHIFT_TOTAL), lambda i: (i, 0)),
            pl.BlockSpec((tm, ng), lambda i: (i, 0)),
        ],
        out_shape=[
            jax.ShapeDtypeStruct((n, CONV_CH), F32),
            jax.ShapeDtypeStruct((n, SHIFT_TOTAL), F32),
            jax.ShapeDtypeStruct((n, ng), F32),
        ],
        compiler_params=_params(("parallel",)),
    )(x2d, g, wc, wr, wg)


CONV_ROWS = 32


def _conv_body(u_ref, halo0_ref, cw_ref, cb_ref, lng_ref, lnb_ref, wo_ref, gate_ref, out_ref,
               ext_ref, y_ref, *, tt):
    t = pl.program_id(1)

    @pl.when(t == 0)
    def _():
        ext_ref[0:HALO, :] = halo0_ref[...]

    @pl.when(t > 0)
    def _():
        ext_ref[0:HALO, :] = ext_ref[tt:tt + HALO, :]

    ext_ref[HALO:HALO + tt, :] = u_ref[0]
    first = HALO - (CONV_KERNEL - 1)
    for r0 in range(0, tt, CONV_ROWS):
        acc = jnp.broadcast_to(cb_ref[...], (CONV_ROWS, CONV_CH))
        for j in range(CONV_KERNEL):
            acc = acc + ext_ref[first + r0 + j:first + r0 + j + CONV_ROWS, :] * cw_ref[j:j + 1, :]
        y_ref[r0:r0 + CONV_ROWS, :] = acc
    y = y_ref[...]
    mean = jnp.mean(y, axis=-1, keepdims=True)
    d = y - mean
    var = jnp.mean(d * d, axis=-1, keepdims=True)
    yn = d * lax.rsqrt(var + LN_EPS) * lng_ref[...] + lnb_ref[...]
    act = yn * _sigmoid(yn)
    ya = jnp.dot(act.astype(BF16), wo_ref[...], preferred_element_type=F32)
    out_ref[0] = ya * gate_ref[0]


def _conv_branch(u, halo0, cw, cb, lng, lnb, wo, gates, tt):
    b, t, _ = u.shape
    d = wo.shape[1]
    return pl.pallas_call(
        functools.partial(_conv_body, tt=tt),
        grid=(b, t // tt),
        in_specs=[
            pl.BlockSpec((1, tt, CONV_CH), lambda i, j: (i, j, 0)),
            _full(halo0.shape), _full(cw.shape), _full(cb.shape), _full(lng.shape), _full(lnb.shape),
            _full(wo.shape),
            pl.BlockSpec((1, tt, d), lambda i, j: (i, j, 0)),
        ],
        out_specs=pl.BlockSpec((1, tt, d), lambda i, j: (i, j, 0)),
        out_shape=jax.ShapeDtypeStruct((b, t, d), F32),
        scratch_shapes=[pltpu.VMEM((tt + HALO, CONV_CH), F32), pltpu.VMEM((tt, CONV_CH), F32)],
        compiler_params=_params(("parallel", "arbitrary")),
    )(u, halo0, cw, cb, lng, lnb, wo, gates)


def _rwkv_prep_body(z_ref, prev0_ref, mu_ref, w0_ref, wup_ref, a0_ref, aup_ref, gup_ref,
                    kk_ref, ka_ref, rk_ref, ones_ref, tri_ref,
                    at_ref, bt_ref, kt_ref, rt_ref, bh_ref, kh_ref, v_ref, gl_ref, g_ref, bonus_ref,
                    ext_ref, *, tt):
    t = pl.program_id(1)
    w = RWKV_WIDTH

    @pl.when(t == 0)
    def _():
        ext_ref[SUBLANES - 1:SUBLANES, :] = prev0_ref[...]

    @pl.when(t > 0)
    def _():
        ext_ref[SUBLANES - 1:SUBLANES, :] = ext_ref[SUBLANES + tt - 1:SUBLANES + tt, :]

    z = z_ref[0]
    ext_ref[SUBLANES:SUBLANES + tt, :] = z
    zprev = ext_ref[SUBLANES - 1:SUBLANES - 1 + tt, :]
    zs = z + (zprev - z) * mu_ref[...]
    r = zs[:, 0:w]
    k = zs[:, w:2 * w]
    v = zs[:, 2 * w:3 * w]
    lwla = zs[:, 3 * w:3 * w + RANK_W + RANK_A]
    lg = zs[:, 3 * w + RANK_W + RANK_A:]
    wlin = w0_ref[...] + jnp.dot(jnp.tanh(lwla).astype(BF16), wup_ref[...], preferred_element_type=F32)
    x = -wlin
    softplus = jnp.maximum(x, 0.0) + jnp.log(1.0 + jnp.exp(-jnp.abs(x)))
    wlog = -softplus - 0.5
    logdecay = -jnp.exp(wlog)
    a = _sigmoid(a0_ref[...] + jnp.dot(lwla.astype(BF16), aup_ref[...], preferred_element_type=F32))
    g = jnp.dot(_sigmoid(lg).astype(BF16), gup_ref[...], preferred_element_type=F32)
    kk = k * kk_ref[...]
    ss = _dot_x3(kk * kk, ones_ref[...])
    kk = kk * lax.rsqrt(jnp.maximum(ss, 1e-24))
    k2 = k * (1.0 + (a - 1.0) * ka_ref[...])
    bonus_ref[0] = _dot_x3(r * k2 * rk_ref[...], ones_ref[...]) * v
    g_ref[0] = g

    cs = _dot_w3(tri_ref[...], logdecay)
    nchunk = tt // CHUNK
    cs3 = cs.reshape(nchunk, CHUNK, w)
    csl3 = cs3[:, CHUNK - 1:CHUNK, :]
    csl = jnp.broadcast_to(csl3, (nchunk, CHUNK, w)).reshape(tt, w)
    gl_ref[0] = jnp.broadcast_to(jnp.exp(csl3), (nchunk, SUBLANES, w))
    inv_g = jnp.exp(-cs)
    to_end = jnp.exp(csl - cs)
    avec = -kk
    bvec = kk * a
    at_ref[0] = avec * jnp.exp(cs - logdecay)
    bt_ref[0] = bvec * inv_g
    kt_ref[0] = k2 * inv_g
    rt_ref[0] = r * jnp.exp(cs)
    bh_ref[0] = bvec * to_end
    kh_ref[0] = k2 * to_end
    v_ref[0] = v


def _rwkv_prep(z, prev0, mu, w0, wup, a0, aup, gup, k_k, k_a, r_k, ones_bd, tri_bd, tt):
    b, t, _ = z.shape
    w = RWKV_WIDTH
    seq = pl.BlockSpec((1, tt, w), lambda i, j: (i, j, 0))
    seq_shape = jax.ShapeDtypeStruct((b, t, w), F32)
    consts = (prev0, mu, w0, wup, a0, aup, gup, k_k, k_a, r_k, ones_bd, tri_bd)
    return pl.pallas_call(
        functools.partial(_rwkv_prep_body, tt=tt),
        grid=(b, t // tt),
        in_specs=[pl.BlockSpec((1, tt, SHIFT_TOTAL), lambda i, j: (i, j, 0))] + [_full(c.shape) for c in consts],
        out_specs=[seq] * 7 + [pl.BlockSpec((1, tt // CHUNK, SUBLANES, w), lambda i, j: (i, j, 0, 0)), seq, seq],
        out_shape=[seq_shape] * 7 + [jax.ShapeDtypeStruct((b, t // CHUNK, SUBLANES, w), F32), seq_shape, seq_shape],
        scratch_shapes=[pltpu.VMEM((tt + SUBLANES, SHIFT_TOTAL), F32)],
        compiler_params=_params(("parallel", "arbitrary")),
    )(z, *consts)


def _chunk_body(at_ref, bt_ref, kt_ref, rt_ref, bh_ref, kh_ref, v_ref, gl_ref,
                rh_ref, o0_ref, p_ref, q_ref):
    gw = GROUP_LANES
    heads = gw // RWKV_HEAD
    rows = heads * CHUNK
    ri = lax.broadcasted_iota(jnp.int32, (rows, rows), 0)
    ci = lax.broadcasted_iota(jnp.int32, (rows, rows), 1)
    same = (ri // CHUNK) == (ci // CHUNK)
    strict = same & (ci < ri)
    incl = same & (ci <= ri)
    eye = ri == ci
    lane = lax.broadcasted_iota(jnp.int32, (1, gw), 1) // RWKV_HEAD
    lri = lax.broadcasted_iota(jnp.int32, (gw, gw), 0)
    lci = lax.broadcasted_iota(jnp.int32, (gw, gw), 1)
    lsame = (lri // RWKV_HEAD) == (lci // RWKV_HEAD)
    leye = lri == lci

    def stack(x):
        return jnp.concatenate([jnp.where(lane == i, x, 0.0) for i in range(heads)], axis=0)

    def unstack(x):
        out = x[0:CHUNK]
        for i in range(1, heads):
            out = out + x[i * CHUNK:(i + 1) * CHUNK]
        return out

    for grp in range(RWKV_WIDTH // gw):
        sl = slice(grp * gw, (grp + 1) * gw)
        at = at_ref[0, :, sl]
        bt = bt_ref[0, :, sl]
        kt = kt_ref[0, :, sl]
        rt = rt_ref[0, :, sl]
        bh = bh_ref[0, :, sl]
        kh = kh_ref[0, :, sl]
        v = v_ref[0, :, sl]
        gl = gl_ref[0, 0, 0:1, sl]
        xa = stack(at)
        xr = stack(rt)
        yb = stack(bt)
        yk = stack(kt)
        vs = stack(v)
        a_ab = jnp.where(strict, _dot_nt(xa, yb), 0.0)
        a_ak = jnp.where(strict, _dot_nt(xa, yk), 0.0)
        a_rb = jnp.where(incl, _dot_nt(xr, yb), 0.0)
        a_rk = jnp.where(incl, _dot_nt(xr, yk), 0.0)
        tm = jnp.where(eye, 1.0, 0.0) + a_ab
        apow = a_ab
        for _ in range(int(math.log2(CHUNK)) - 1):
            apow = _dot(apow, apow)
            tm = tm + _dot(tm, apow)
        akv = _dot(a_ak, vs)
        tw = _dot(tm, jnp.concatenate([xa, akv], axis=1))
        ahat_s = tw[:, :gw]
        u0_s = tw[:, gw:]
        rb = _dot(a_rb, tw)
        rhat = unstack(xr + rb[:, :gw])
        o0 = unstack(_dot(a_rk, vs) + rb[:, gw:])
        ahat = unstack(ahat_s)
        u0 = unstack(u0_s)
        pm = jnp.where(lsame, _dot_tn(ahat, bh), 0.0) + jnp.where(leye, jnp.broadcast_to(gl, (gw, gw)), 0.0)
        qm = jnp.where(lsame, _dot_tn(u0, bh) + _dot_tn(v, kh), 0.0)
        rh_ref[0, :, sl] = rhat
        o0_ref[0, :, sl] = o0
        for pr in range(gw // PAIR_LANES):
            ps = slice(pr * PAIR_LANES, (pr + 1) * PAIR_LANES)
            p_ref[0, 0, grp * (gw // PAIR_LANES) + pr] = pm[ps, ps]
            q_ref[0, 0, grp * (gw // PAIR_LANES) + pr] = qm[ps, ps]


def _chunk_transforms(at, bt, kt, rt, bh, kh, v, gl):
    b, t, w = at.shape
    nc = t // CHUNK
    npair = w // PAIR_LANES
    seq = pl.BlockSpec((1, CHUNK, w), lambda i, j: (i, j, 0))
    mat = pl.BlockSpec((1, 1, npair, PAIR_LANES, PAIR_LANES), lambda i, j: (i, j, 0, 0, 0))
    return pl.pallas_call(
        _chunk_body,
        grid=(b, nc),
        in_specs=[seq] * 7 + [pl.BlockSpec((1, 1, SUBLANES, w), lambda i, j: (i, j, 0, 0))],
        out_specs=[seq, seq, mat, mat],
        out_shape=[
            jax.ShapeDtypeStruct((b, t, w), F32),
            jax.ShapeDtypeStruct((b, t, w), F32),
            jax.ShapeDtypeStruct((b, nc, npair, PAIR_LANES, PAIR_LANES), F32),
            jax.ShapeDtypeStruct((b, nc, npair, PAIR_LANES, PAIR_LANES), F32),
        ],
        compiler_params=_params(("parallel", "parallel")),
    )(at, bt, kt, rt, bh, kh, v, gl)


def _scan_body(rh_ref, o0_ref, p_ref, q_ref, s0_ref, o_ref, sfin_ref, s_ref):
    c = pl.program_id(0)
    nb, npair = s_ref.shape[0], s_ref.shape[1]

    @pl.when(c == 0)
    def _():
        for bi in range(nb):
            s_ref[bi] = s0_ref[...]

    hp = lax.Precision.HIGHEST
    for bi in range(nb):
        for pr in range(npair):
            ps = slice(pr * PAIR_LANES, (pr + 1) * PAIR_LANES)
            s = s_ref[bi, pr]
            o_ref[bi, :, ps] = (
                lax.dot_general(rh_ref[bi, :, ps], s, NT_DIMS, precision=hp, preferred_element_type=F32)
                + o0_ref[bi, :, ps])
            s_ref[bi, pr] = jnp.dot(s, p_ref[bi, 0, pr], precision=hp, preferred_element_type=F32) + q_ref[bi, 0, pr]

    @pl.when(c == pl.num_programs(0) - 1)
    def _():
        sfin_ref[...] = s_ref[...]


def _state_scan(rh, o0, p, q, s0):
    b, t, w = rh.shape
    nc = t // CHUNK
    npair = w // PAIR_LANES
    seq = pl.BlockSpec((b, CHUNK, w), lambda c: (0, c, 0))
    mat = pl.BlockSpec((b, 1, npair, PAIR_LANES, PAIR_LANES), lambda c: (0, c, 0, 0, 0))
    return pl.pallas_call(
        _scan_body,
        grid=(nc,),
        in_specs=[seq, seq, mat, mat, _full(s0.shape)],
        out_specs=[seq, _full((b, npair, PAIR_LANES, PAIR_LANES))],
        out_shape=[
            jax.ShapeDtypeStruct((b, t, w), F32),
            jax.ShapeDtypeStruct((b, npair, PAIR_LANES, PAIR_LANES), F32),
        ],
        scratch_shapes=[pltpu.VMEM((b, npair, PAIR_LANES, PAIR_LANES), F32)],
        compiler_params=_params(("arbitrary",)),
    )(rh, o0, p, q, s0)


def _merge_body(o_ref, bonus_ref, g_ref, gng_ref, gnb_ref, avg_ref, wro_ref, yag_ref, gateb_ref, wo_ref, x_ref,
                h_ref):
    o = o_ref[...]
    mean = _dot_x3(o, avg_ref[...])
    d = o - mean
    var = _dot_x3(d * d, avg_ref[...])
    on = d * lax.rsqrt(var + GN_EPS) * gng_ref[...] + gnb_ref[...]
    xo = (on + bonus_ref[...]) * g_ref[...]
    yb = jnp.dot(xo.astype(BF16), wro_ref[...], preferred_element_type=F32)
    m = yag_ref[...] + gateb_ref[...] * yb
    h_ref[...] = x_ref[...] + jnp.dot(m.astype(BF16), wo_ref[...], preferred_element_type=F32)


def _merge(o, bonus, g, gng, gnb, avg_bd, wro, yag, gates, wo, x2d, tm):
    n, d = x2d.shape
    w = RWKV_WIDTH
    row_w = pl.BlockSpec((tm, w), lambda i: (i, 0))
    row_d = pl.BlockSpec((tm, d), lambda i: (i, 0))
    return pl.pallas_call(
        _merge_body,
        grid=(n // tm,),
        in_specs=[row_w, row_w, row_w, _full(gng.shape), _full(gnb.shape), _full(avg_bd.shape), _full(wro.shape),
                  row_d, pl.BlockSpec((tm, d), lambda i: (i, 1)), _full(wo.shape), row_d],
        out_specs=row_d,
        out_shape=jax.ShapeDtypeStruct((n, d), F32),
        compiler_params=_params(("parallel",)),
    )(o, bonus, g, gng, gnb, avg_bd, wro, yag, gates, wo, x2d)


def _candidate_pairs():
    return [(i, j) for i in range(PEER_TOPK) for j in range(PEER_TOPK) if (i + 1) * (j + 1) <= PEER_TOPK]


N_CAND = len(_candidate_pairs())
N_CAND_PAD = -(-N_CAND // (2 * SUBLANES)) * (2 * SUBLANES)


def _peer_body(h_ref, gffn_ref, wqt_ref, keys_ref, sela_ref, selb_ref, u_ref, vt_ref, gfin_ref, out_ref,
               xn_ref, qt_ref, s1_ref, e1_ref, s2_ref, e2_ref, tau_ref, top_ref, act_ref, c_ref, yt_ref,
               *, tb, i1c):
    c = pl.program_id(1)
    ncol = tb // LANES
    neg_inf = float("-inf")

    def top_values(s):
        vals = []
        cur = s
        for _ in range(PEER_TOPK):
            m = jnp.max(cur, axis=0, keepdims=True)
            vals.append(m)
            cur = jnp.where(cur == m, neg_inf, cur)
        return vals

    @pl.when(c == 0)
    def _prep():
        xn = _rmsnorm(h_ref[...], gffn_ref[...]).astype(BF16)
        xn_ref[...] = xn
        qt_ref[...] = lax.dot_general(wqt_ref[...], xn, NT_DIMS, preferred_element_type=F32).astype(BF16)
        yt_ref[...] = jnp.zeros_like(yt_ref)

        def per_head(hh, carry):
            base = pl.multiple_of(hh * (2 * PEER_DHALF), 2 * PEER_DHALF)
            s1 = jnp.dot(keys_ref[hh, 0], qt_ref[pl.ds(base, PEER_DHALF), :], preferred_element_type=F32)
            s2 = jnp.dot(keys_ref[hh, 1], qt_ref[pl.ds(base + PEER_DHALF, PEER_DHALF), :],
                         preferred_element_type=F32)
            for lc in range(ncol):
                ls = slice(lc * LANES, (lc + 1) * LANES)
                s1c = s1[:, ls]
                s2c = s2[:, ls]
                va = top_values(s1c)
                vb = top_values(s2c)
                for i in range(PEER_TOPK):
                    top_ref[0, i:i + 1, :] = va[i]
                    top_ref[1, i:i + 1, :] = vb[i]
                row = lax.broadcasted_iota(jnp.int32, (N_CAND_PAD, LANES), 0)
                cand = _dot_w3(sela_ref[...], top_ref[0]) + _dot_w3(selb_ref[...], top_ref[1])
                cand = jnp.where(row < N_CAND, cand, neg_inf)
                best = top_values(cand)
                mx = best[0]
                zsum = jnp.zeros_like(mx)
                for bv in best:
                    zsum = zsum + jnp.exp(bv - mx)
                grouped = (PEER_NKEYS // SUBLANES, SUBLANES, LANES)
                tau_ref[hh, lc] = jnp.broadcast_to(best[PEER_TOPK - 1], (SUBLANES, LANES))
                s1_ref[hh, lc] = s1c.reshape(grouped)
                e1_ref[hh, lc] = jnp.exp(s1c - va[0]).reshape(grouped)
                s2_ref[hh, lc] = s2c
                e2_ref[hh, lc] = jnp.exp(s2c - vb[0]) / zsum
            return carry

        lax.fori_loop(0, PEER_HEADS, per_head, 0)

    a_full = lax.dot_general(u_ref[...], xn_ref[...], NT_DIMS, preferred_element_type=F32)
    for lc in range(ncol):
        act_ref[lc] = a_full[:, lc * LANES:(lc + 1) * LANES]

    def per_col(lc, carry):
        for i1 in range(i1c):
            rows = slice(i1 * PEER_NKEYS, (i1 + 1) * PEER_NKEYS)
            acc = jnp.zeros((PEER_NKEYS, LANES), F32)
            for hh in range(PEER_HEADS):
                s1row = s1_ref[hh, lc, c, i1:i1 + 1, :]
                e1row = e1_ref[hh, lc, c, i1:i1 + 1, :]
                keep = (s1row + s2_ref[hh, lc]) >= tau_ref[hh, lc, 0:1, :]
                acc = acc + jnp.where(keep, e1row * e2_ref[hh, lc], 0.0)
            a = act_ref[lc, rows, :]
            gelu = 0.5 * a * (1.0 + lax.erf(a * math.sqrt(0.5)))
            c_ref[lc, rows, :] = (acc * gelu).astype(BF16)
        return carry

    lax.fori_loop(0, ncol, per_col, 0)
    c_full = jnp.concatenate([c_ref[lc] for lc in range(ncol)], axis=1)
    yt_ref[...] += jnp.dot(vt_ref[...], c_full, preferred_element_type=F32)

    @pl.when(c == pl.num_programs(1) - 1)
    def _fin():
        hh = h_ref[...] + yt_ref[...].T
        out_ref[...] = _rmsnorm(hh, gfin_ref[...])


def _peer(h1, gffn, wqt, keys, sela, selb, u, vt, gfin, tb, i1c):
    n, d = h1.shape
    ne = u.shape[0]
    assert i1c == SUBLANES
    ec = i1c * PEER_NKEYS
    nq = wqt.shape[0]
    ncol = tb // LANES
    return pl.pallas_call(
        functools.partial(_peer_body, tb=tb, i1c=i1c),
        grid=(n // tb, ne // ec),
        in_specs=[
            pl.BlockSpec((tb, d), lambda i, c: (i, 0)),
            _full(gffn.shape), _full(wqt.shape), _full(keys.shape), _full(sela.shape), _full(selb.shape),
            pl.BlockSpec((ec, d), lambda i, c: (c, 0)),
            pl.BlockSpec((d, ec), lambda i, c: (0, c)),
            _full(gfin.shape),
        ],
        out_specs=pl.BlockSpec((tb, d), lambda i, c: (i, 0)),
        out_shape=jax.ShapeDtypeStruct((n, d), F32),
        scratch_shapes=[
            pltpu.VMEM((tb, d), BF16),
            pltpu.VMEM((nq, tb), BF16),
            pltpu.VMEM((PEER_HEADS, ncol, PEER_NKEYS // SUBLANES, SUBLANES, LANES), F32),
            pltpu.VMEM((PEER_HEADS, ncol, PEER_NKEYS // SUBLANES, SUBLANES, LANES), F32),
            pltpu.VMEM((PEER_HEADS, ncol, PEER_NKEYS, LANES), F32),
            pltpu.VMEM((PEER_HEADS, ncol, PEER_NKEYS, LANES), F32),
            pltpu.VMEM((PEER_HEADS, ncol, SUBLANES, LANES), F32),
            pltpu.VMEM((2, PEER_TOPK, LANES), F32),
            pltpu.VMEM((ncol, ec, LANES), F32),
            pltpu.VMEM((ncol, ec, LANES), BF16),
            pltpu.VMEM((d, tb), F32),
        ],
        compiler_params=_params(("parallel", "arbitrary")),
    )(h1, gffn, wqt, keys, sela, selb, u, vt, gfin)


def _block_diag_const(width, block, value):
    idx = jnp.arange(width) // block
    return jnp.where(idx[:, None] == idx[None, :], value, 0.0).astype(BF16)


def _tile(n, pref):
    t = min(n, pref)
    assert n % t == 0, (n, pref)
    return t


def kernel(x, meta_tokens, g_mix, w_in, conv_w, conv_b, conv_ln_g, conv_ln_b, w_conv_out, mu_shift, w0, w_up, a0, a_up, g_up, k_k, k_a, r_k, gn_g, gn_b, w_rwkv_out, w_o, g_ffn, w_q, sub_keys, expert_u, expert_v, g_final):
    depth = g_mix.shape[0]
    assert depth == 1
    b, t, d = x.shape
    n = b * t
    n_meta = meta_tokens.shape[0]
    assert n_meta <= min(HALO, CHUNK) and n_meta % SUBLANES == 0
    w = RWKV_WIDTH
    row = lambda p: p.reshape(1, -1).astype(F32)

    w_in0 = w_in[0]
    wc = w_in0[:, :2 * CONV_CH].astype(BF16)
    wr = w_in0[:, 2 * CONV_CH:2 * CONV_CH + SHIFT_TOTAL].astype(BF16)
    wg = w_in0[:, 2 * CONV_CH + SHIFT_TOTAL:].astype(BF16)
    g_mix0 = row(g_mix[0])
    zeros_w = jnp.zeros((RANK_W, w), F32)
    wup_pad = jnp.concatenate([w_up[0], jnp.zeros((RANK_A, w), F32)], axis=0).astype(BF16)
    aup_pad = jnp.concatenate([zeros_w, a_up[0]], axis=0).astype(BF16)
    ones_bd = _block_diag_const(w, RWKV_HEAD, 1.0)
    avg_bd = _block_diag_const(w, RWKV_HEAD, 1.0 / RWKV_HEAD)
    cw = jnp.concatenate([conv_w[0], jnp.zeros((1, CONV_CH), F32)], axis=0)

    x2d = x.reshape(n, d)
    u, zr, gates = _inproj(x2d, g_mix0, wc, wr, wg, _tile(n, 256))
    u_m, zr_m, _ = _inproj(meta_tokens.astype(F32), g_mix0, wc, wr, wg, n_meta)

    halo0 = jnp.concatenate([jnp.zeros((HALO - n_meta, CONV_CH), F32), u_m], axis=0)
    tt = _tile(t, 256)
    yag = _conv_branch(u.reshape(b, t, CONV_CH), halo0, cw, row(conv_b[0]), row(conv_ln_g[0]), row(conv_ln_b[0]),
                       w_conv_out[0].astype(BF16), gates.reshape(b, t, 2 * d), tt)

    tri = jnp.tril(jnp.ones((CHUNK, CHUNK), F32))
    prep_consts = (row(mu_shift[0]), row(w0[0]), wup_pad, row(a0[0]), aup_pad, g_up[0].astype(BF16),
                   row(k_k[0]), row(k_a[0]), row(r_k[0]), ones_bd)

    def rwkv(z, prev0, s0, tt_):
        tri_bd = jnp.kron(jnp.eye(tt_ // CHUNK, dtype=F32), tri).astype(BF16)
        outs = _rwkv_prep(z, prev0, *prep_consts, tri_bd, tt_)
        at, bt, kt, rt, bh, kh, v, gl, g, bonus = outs
        rh, o0, p, q = _chunk_transforms(at, bt, kt, rt, bh, kh, v, gl)
        o, sfin = _state_scan(rh, o0, p, q, s0)
        return o, g, bonus, sfin

    z_meta = jnp.concatenate([jnp.zeros((CHUNK - n_meta, SHIFT_TOTAL), F32), zr_m], axis=0)[None]
    s_zero = jnp.zeros((w // PAIR_LANES, PAIR_LANES, PAIR_LANES), F32)
    _, _, _, s_meta = rwkv(z_meta, jnp.zeros((1, SHIFT_TOTAL), F32), s_zero, CHUNK)
    o, g, bonus, _ = rwkv(zr.reshape(b, t, SHIFT_TOTAL), zr_m[n_meta - 1:n_meta], s_meta[0], tt)

    h1 = _merge(o.reshape(n, w), bonus.reshape(n, w), g.reshape(n, w), row(gn_g[0]), row(gn_b[0]), avg_bd,
                w_rwkv_out[0].astype(BF16), yag.reshape(n, d), gates, w_o[0].astype(BF16), x2d, _tile(n, 256))

    pairs = _candidate_pairs()
    sela = jnp.zeros((N_CAND_PAD, PEER_TOPK), F32).at[jnp.arange(N_CAND), jnp.array([p[0] for p in pairs])].set(1.0)
    selb = jnp.zeros((N_CAND_PAD, PEER_TOPK), F32).at[jnp.arange(N_CAND), jnp.array([p[1] for p in pairs])].set(1.0)
    out = _peer(h1, row(g_ffn[0]), w_q[0].T.astype(BF16), sub_keys[0].astype(BF16),
                sela.astype(BF16), selb.astype(BF16), expert_u[0].astype(BF16), expert_v[0].T.astype(BF16),
                row(g_final), _tile(n, 512), 8)
    return out.reshape(b, t, d)
```

```python
import functools
import math

import jax
import jax.numpy as jnp
from jax import lax
from jax.experimental import pallas as pl
from jax.experimental.pallas import tpu as pltpu

F32 = jnp.float32
BF16 = jnp.bfloat16

CONV_CH = 512
CONV_KERNEL = 31
RWKV_WIDTH = 512
RWKV_HEAD = 64
RWKV_HEADS = RWKV_WIDTH // RWKV_HEAD
RANK_W = 64
RANK_A = 64
RANK_G = 128
SHIFT_TOTAL = 3 * RWKV_WIDTH + RANK_W + RANK_A + RANK_G
PEER_HEADS = 8
PEER_NKEYS = 128
PEER_DHALF = 128
PEER_TOPK = 16
RMS_EPS = 1e-6
LN_EPS = 1e-5
GN_EPS = 64e-5

LANES = 128
SUBLANES = 8
VMEM_LIMIT_BYTES = 56 * 1024 * 1024

CHUNK = 64
HALO = 32
GROUP_LANES = 256
PAIR_LANES = 128

NT_DIMS = (((1,), (1,)), ((), ()))
TN_DIMS = (((0,), (0,)), ((), ()))


def _sigmoid(x):
    return 1.0 / (1.0 + jnp.exp(-x))


def _dot(a, b):
    return jnp.dot(a.astype(BF16), b.astype(BF16), preferred_element_type=F32)


def _dot_nt(a, b):
    return lax.dot_general(a.astype(BF16), b.astype(BF16), NT_DIMS, preferred_element_type=F32)


def _dot_tn(a, b):
    return lax.dot_general(a.astype(BF16), b.astype(BF16), TN_DIMS, preferred_element_type=F32)


def _split3(x):
    hi = x.astype(BF16)
    r1 = x - hi.astype(F32)
    mid = r1.astype(BF16)
    lo = (r1 - mid.astype(F32)).astype(BF16)
    return hi, mid, lo


def _dot_x3(x, w_bf16):
    hi, mid, lo = _split3(x)
    acc = jnp.dot(hi, w_bf16, preferred_element_type=F32)
    acc = acc + jnp.dot(mid, w_bf16, preferred_element_type=F32)
    return acc + jnp.dot(lo, w_bf16, preferred_element_type=F32)


def _dot_w3(w_bf16, x):
    hi, mid, lo = _split3(x)
    acc = jnp.dot(w_bf16, hi, preferred_element_type=F32)
    acc = acc + jnp.dot(w_bf16, mid, preferred_element_type=F32)
    return acc + jnp.dot(w_bf16, lo, preferred_element_type=F32)


def _rmsnorm(x, g):
    ms = jnp.mean(x * x, axis=-1, keepdims=True)
    return x * lax.rsqrt(ms + RMS_EPS) * g


def _params(semantics):
    return pltpu.CompilerParams(dimension_semantics=semantics, vmem_limit_bytes=VMEM_LIMIT_BYTES)


def _full(shape):
    nd = len(shape)
    return pl.BlockSpec(shape, lambda *_: (0,) * nd)


def _inproj_body(x_ref, g_ref, wc_ref, wr_ref, wg_ref, u_ref, zr_ref, gate_ref):
    xn = _rmsnorm(x_ref[...], g_ref[...]).astype(BF16)
    zc = jnp.dot(xn, wc_ref[...], preferred_element_type=F32)
    u_ref[...] = zc[:, :CONV_CH] * _sigmoid(zc[:, CONV_CH:])
    zr_ref[...] = jnp.dot(xn, wr_ref[...], preferred_element_type=F32)
    gate_ref[...] = _sigmoid(jnp.dot(xn, wg_ref[...], preferred_element_type=F32))


def _inproj(x2d, g, wc, wr, wg, tm):
    n, d = x2d.shape
    ng = wg.shape[1]
    return pl.pallas_call(
        _inproj_body,
        grid=(n // tm,),
        in_specs=[
            pl.BlockSpec((tm, d), lambda i: (i, 0)),
            _full(g.shape), _full(wc.shape), _full(wr.shape), _full(wg.shape),
        ],
        out_specs=[
            pl.BlockSpec((tm, CONV_CH), lambda i: (i, 0)),
            pl.BlockSpec((tm, SHIFT_TOTAL), lambda i: (i, 0)),
            pl.BlockSpec((tm, ng), lambda i: (i, 0)),
        ],
        out_shape=[
            jax.ShapeDtypeStruct((n, CONV_CH), F32),
            jax.ShapeDtypeStruct((n, SHIFT_TOTAL), F32),
            jax.ShapeDtypeStruct((n, ng), F32),
        ],
        compiler_params=_params(("parallel",)),
    )(x2d, g, wc, wr, wg)


CONV_ROWS = 32


def _conv_body(u_ref, halo0_ref, cw_ref, cb_ref, lng_ref, lnb_ref, wo_ref, gate_ref, out_ref,
               ext_ref, y_ref, *, tt):
    t = pl.program_id(1)

    @pl.when(t == 0)
    def _():
        ext_ref[0:HALO, :] = halo0_ref[...]

    @pl.when(t > 0)
    def _():
        ext_ref[0:HALO, :] = ext_ref[tt:tt + HALO, :]

    ext_ref[HALO:HALO + tt, :] = u_ref[0]
    first = HALO - (CONV_KERNEL - 1)
    for r0 in range(0, tt, CONV_ROWS):
        acc = jnp.broadcast_to(cb_ref[...], (CONV_ROWS, CONV_CH))
        for j in range(CONV_KERNEL):
            acc = acc + ext_ref[first + r0 + j:first + r0 + j + CONV_ROWS, :] * cw_ref[j:j + 1, :]
        y_ref[r0:r0 + CONV_ROWS, :] = acc
    y = y_ref[...]
    mean = jnp.mean(y, axis=-1, keepdims=True)
    d = y - mean
    var = jnp.mean(d * d, axis=-1, keepdims=True)
    yn = d * lax.rsqrt(var + LN_EPS) * lng_ref[...] + lnb_ref[...]
    act = yn * _sigmoid(yn)
    ya = jnp.dot(act.astype(BF16), wo_ref[...], preferred_element_type=F32)
    out_ref[0] = ya * gate_ref[0]


def _conv_branch(u, halo0, cw, cb, lng, lnb, wo, gates, tt):
    b, t, _ = u.shape
    d = wo.shape[1]
    return pl.pallas_call(
        functools.partial(_conv_body, tt=tt),
        grid=(b, t // tt),
        in_specs=[
            pl.BlockSpec((1, tt, CONV_CH), lambda i, j: (i, j, 0)),
            _full(halo0.shape), _full(cw.shape), _full(cb.shape), _full(lng.shape), _full(lnb.shape),
            _full(wo.shape),
            pl.BlockSpec((1, tt, d), lambda i, j: (i, j, 0)),
        ],
        out_specs=pl.BlockSpec((1, tt, d), lambda i, j: (i, j, 0)),
        out_shape=jax.ShapeDtypeStruct((b, t, d), F32),
        scratch_shapes=[pltpu.VMEM((tt + HALO, CONV_CH), F32), pltpu.VMEM((tt, CONV_CH), F32)],
        compiler_params=_params(("parallel", "arbitrary")),
    )(u, halo0, cw, cb, lng, lnb, wo, gates)


def _rwkv_prep_body(z_ref, prev0_ref, mu_ref, w0_ref, wup_ref, a0_ref, aup_ref, gup_ref,
                    kk_ref, ka_ref, rk_ref, ones_ref, tri_ref,
                    at_ref, bt_ref, kt_ref, rt_ref, bh_ref, kh_ref, v_ref, gl_ref, g_ref, bonus_ref,
                    ext_ref, *, tt):
    t = pl.program_id(1)
    w = RWKV_WIDTH

    @pl.when(t == 0)
    def _():
        ext_ref[SUBLANES - 1:SUBLANES, :] = prev0_ref[...]

    @pl.when(t > 0)
    def _():
        ext_ref[SUBLANES - 1:SUBLANES, :] = ext_ref[SUBLANES + tt - 1:SUBLANES + tt, :]

    z = z_ref[0]
    ext_ref[SUBLANES:SUBLANES + tt, :] = z
    zprev = ext_ref[SUBLANES - 1:SUBLANES - 1 + tt, :]
    zs = z + (zprev - z) * mu_ref[...]
    r = zs[:, 0:w]
    k = zs[:, w:2 * w]
    v = zs[:, 2 * w:3 * w]
    lwla = zs[:, 3 * w:3 * w + RANK_W + RANK_A]
    lg = zs[:, 3 * w + RANK_W + RANK_A:]
    wlin = w0_ref[...] + jnp.dot(jnp.tanh(lwla).astype(BF16), wup_ref[...], preferred_element_type=F32)
    x = -wlin
    softplus = jnp.maximum(x, 0.0) + jnp.log(1.0 + jnp.exp(-jnp.abs(x)))
    wlog = -softplus - 0.5
    logdecay = -jnp.exp(wlog)
    a = _sigmoid(a0_ref[...] + jnp.dot(lwla.astype(BF16), aup_ref[...], preferred_element_type=F32))
    g = jnp.dot(_sigmoid(lg).astype(BF16), gup_ref[...], preferred_element_type=F32)
    kk = k * kk_ref[...]
    ss = _dot_x3(kk * kk, ones_ref[...])
    kk = kk * lax.rsqrt(jnp.maximum(ss, 1e-24))
    k2 = k * (1.0 + (a - 1.0) * ka_ref[...])
    bonus_ref[0] = _dot_x3(r * k2 * rk_ref[...], ones_ref[...]) * v
    g_ref[0] = g

    cs = _dot_w3(tri_ref[...], logdecay)
    nchunk = tt // CHUNK
    cs3 = cs.reshape(nchunk, CHUNK, w)
    csl3 = cs3[:, CHUNK - 1:CHUNK, :]
    csl = jnp.broadcast_to(csl3, (nchunk, CHUNK, w)).reshape(tt, w)
    gl_ref[0] = jnp.broadcast_to(jnp.exp(csl3), (nchunk, SUBLANES, w))
    inv_g = jnp.exp(-cs)
    to_end = jnp.exp(csl - cs)
    avec = -kk
    bvec = kk * a
    at_ref[0] = avec * jnp.exp(cs - logdecay)
    bt_ref[0] = bvec * inv_g
    kt_ref[0] = k2 * inv_g
    rt_ref[0] = r * jnp.exp(cs)
    bh_ref[0] = bvec * to_end
    kh_ref[0] = k2 * to_end
    v_ref[0] = v


def _rwkv_prep(z, prev0, mu, w0, wup, a0, aup, gup, k_k, k_a, r_k, ones_bd, tri_bd, tt):
    b, t, _ = z.shape
    w = RWKV_WIDTH
    seq = pl.BlockSpec((1, tt, w), lambda i, j: (i, j, 0))
    seq_shape = jax.ShapeDtypeStruct((b, t, w), F32)
    consts = (prev0, mu, w0, wup, a0, aup, gup, k_k, k_a, r_k, ones_bd, tri_bd)
    return pl.pallas_call(
        functools.partial(_rwkv_prep_body, tt=tt),
        grid=(b, t // tt),
        in_specs=[pl.BlockSpec((1, tt, SHIFT_TOTAL), lambda i, j: (i, j, 0))] + [_full(c.shape) for c in consts],
        out_specs=[seq] * 7 + [pl.BlockSpec((1, tt // CHUNK, SUBLANES, w), lambda i, j: (i, j, 0, 0)), seq, seq],
        out_shape=[seq_shape] * 7 + [jax.ShapeDtypeStruct((b, t // CHUNK, SUBLANES, w), F32), seq_shape, seq_shape],
        scratch_shapes=[pltpu.VMEM((tt + SUBLANES, SHIFT_TOTAL), F32)],
        compiler_params=_params(("parallel", "arbitrary")),
    )(z, *consts)


def _chunk_body(at_ref, bt_ref, kt_ref, rt_ref, bh_ref, kh_ref, v_ref, gl_ref,
                rh_ref, o0_ref, p_ref, q_ref, *, cps):
    gw = GROUP_LANES
    heads = gw // RWKV_HEAD
    rows = heads * CHUNK
    ri = lax.broadcasted_iota(jnp.int32, (rows, rows), 0)
    ci = lax.broadcasted_iota(jnp.int32, (rows, rows), 1)
    same = (ri // CHUNK) == (ci // CHUNK)
    strict = same & (ci < ri)
    incl = same & (ci <= ri)
    eye = ri == ci
    lane = lax.broadcasted_iota(jnp.int32, (1, gw), 1) // RWKV_HEAD
    lri = lax.broadcasted_iota(jnp.int32, (gw, gw), 0)
    lci = lax.broadcasted_iota(jnp.int32, (gw, gw), 1)
    lsame = (lri // RWKV_HEAD) == (lci // RWKV_HEAD)
    leye = lri == lci

    def stack(x):
        return jnp.concatenate([jnp.where(lane == i, x, 0.0) for i in range(heads)], axis=0)

    def unstack(x):
        out = x[0:CHUNK]
        for i in range(1, heads):
            out = out + x[i * CHUNK:(i + 1) * CHUNK]
        return out

    chains = [(ck, grp) for ck in range(cps) for grp in range(RWKV_WIDTH // gw)]

    def load(ref):
        return [ref[0, ck * CHUNK:(ck + 1) * CHUNK, grp * gw:(grp + 1) * gw] for ck, grp in chains]

    def each(fn, *lists):
        return [fn(*args) for args in zip(*lists)]

    at, bt, kt, rt, bh, kh, v = (load(r) for r in (at_ref, bt_ref, kt_ref, rt_ref, bh_ref, kh_ref, v_ref))
    xa, xr, yb, yk, vs = (each(stack, x) for x in (at, rt, bt, kt, v))
    a_ab = each(lambda x, y: jnp.where(strict, _dot_nt(x, y), 0.0), xa, yb)
    a_ak = each(lambda x, y: jnp.where(strict, _dot_nt(x, y), 0.0), xa, yk)
    a_rb = each(lambda x, y: jnp.where(incl, _dot_nt(x, y), 0.0), xr, yb)
    a_rk = each(lambda x, y: jnp.where(incl, _dot_nt(x, y), 0.0), xr, yk)
    tm = each(lambda a: jnp.where(eye, 1.0, 0.0) + a, a_ab)
    apow = a_ab
    for _ in range(int(math.log2(CHUNK)) - 1):
        apow = each(lambda a: _dot(a, a), apow)
        tm = each(lambda t, a: t + _dot(t, a), tm, apow)
    akv = each(_dot, a_ak, vs)
    tw = each(lambda t, x, y: _dot(t, jnp.concatenate([x, y], axis=1)), tm, xa, akv)
    rb = each(_dot, a_rb, tw)
    ov = each(_dot, a_rk, vs)
    rhat = each(lambda x, r: unstack(x + r[:, :gw]), xr, rb)
    o0 = each(lambda o, r: unstack(o + r[:, gw:]), ov, rb)
    ahat = each(lambda t: unstack(t[:, :gw]), tw)
    u0 = each(lambda t: unstack(t[:, gw:]), tw)
    pm = each(lambda a, b: jnp.where(lsame, _dot_tn(a, b), 0.0), ahat, bh)
    qm = each(lambda u, b, vv, k: jnp.where(lsame, _dot_tn(u, b) + _dot_tn(vv, k), 0.0), u0, bh, v, kh)
    for i, (ck, grp) in enumerate(chains):
        sl = slice(grp * gw, (grp + 1) * gw)
        tr = slice(ck * CHUNK, (ck + 1) * CHUNK)
        gl = gl_ref[0, ck, 0:1, sl]
        pmi = pm[i] + jnp.where(leye, jnp.broadcast_to(gl, (gw, gw)), 0.0)
        rh_ref[0, tr, sl] = rhat[i]
        o0_ref[0, tr, sl] = o0[i]
        for pr in range(gw // PAIR_LANES):
            ps = slice(pr * PAIR_LANES, (pr + 1) * PAIR_LANES)
            p_ref[0, ck, grp * (gw // PAIR_LANES) + pr] = pmi[ps, ps]
            q_ref[0, ck, grp * (gw // PAIR_LANES) + pr] = qm[i][ps, ps]


def _chunk_transforms(at, bt, kt, rt, bh, kh, v, gl):
    b, t, w = at.shape
    nc = t // CHUNK
    cps = 2 if nc % 2 == 0 else 1
    npair = w // PAIR_LANES
    seq = pl.BlockSpec((1, cps * CHUNK, w), lambda i, j: (i, j, 0))
    mat = pl.BlockSpec((1, cps, npair, PAIR_LANES, PAIR_LANES), lambda i, j: (i, j, 0, 0, 0))
    return pl.pallas_call(
        functools.partial(_chunk_body, cps=cps),
        grid=(b, nc // cps),
        in_specs=[seq] * 7 + [pl.BlockSpec((1, cps, SUBLANES, w), lambda i, j: (i, j, 0, 0))],
        out_specs=[seq, seq, mat, mat],
        out_shape=[
            jax.ShapeDtypeStruct((b, t, w), F32),
            jax.ShapeDtypeStruct((b, t, w), F32),
            jax.ShapeDtypeStruct((b, nc, npair, PAIR_LANES, PAIR_LANES), F32),
            jax.ShapeDtypeStruct((b, nc, npair, PAIR_LANES, PAIR_LANES), F32),
        ],
        compiler_params=_params(("parallel", "parallel")),
    )(at, bt, kt, rt, bh, kh, v, gl)


def _scan_body(rh_ref, o0_ref, p_ref, q_ref, s0_ref, o_ref, sfin_ref, s_ref):
    c = pl.program_id(0)
    nb, npair = s_ref.shape[0], s_ref.shape[1]

    @pl.when(c == 0)
    def _():
        for bi in range(nb):
            s_ref[bi] = s0_ref[...]

    hp = lax.Precision.HIGHEST
    for bi in range(nb):
        for pr in range(npair):
            ps = slice(pr * PAIR_LANES, (pr + 1) * PAIR_LANES)
            s = s_ref[bi, pr]
            o_ref[bi, :, ps] = (
                lax.dot_general(rh_ref[bi, :, ps], s, NT_DIMS, precision=hp, preferred_element_type=F32)
                + o0_ref[bi, :, ps])
            s_ref[bi, pr] = jnp.dot(s, p_ref[bi, 0, pr], precision=hp, preferred_element_type=F32) + q_ref[bi, 0, pr]

    @pl.when(c == pl.num_programs(0) - 1)
    def _():
        sfin_ref[...] = s_ref[...]


def _state_scan(rh, o0, p, q, s0):
    b, t, w = rh.shape
    nc = t // CHUNK
    npair = w // PAIR_LANES
    seq = pl.BlockSpec((b, CHUNK, w), lambda c: (0, c, 0))
    mat = pl.BlockSpec((b, 1, npair, PAIR_LANES, PAIR_LANES), lambda c: (0, c, 0, 0, 0))
    return pl.pallas_call(
        _scan_body,
        grid=(nc,),
        in_specs=[seq, seq, mat, mat, _full(s0.shape)],
        out_specs=[seq, _full((b, npair, PAIR_LANES, PAIR_LANES))],
        out_shape=[
            jax.ShapeDtypeStruct((b, t, w), F32),
            jax.ShapeDtypeStruct((b, npair, PAIR_LANES, PAIR_LANES), F32),
        ],
        scratch_shapes=[pltpu.VMEM((b, npair, PAIR_LANES, PAIR_LANES), F32)],
        compiler_params=_params(("arbitrary",)),
    )(rh, o0, p, q, s0)


def _merge_body(o_ref, bonus_ref, g_ref, gng_ref, gnb_ref, avg_ref, wro_ref, yag_ref, gateb_ref, wo_ref, x_ref,
                h_ref):
    o = o_ref[...]
    mean = _dot_x3(o, avg_ref[...])
    d = o - mean
    var = _dot_x3(d * d, avg_ref[...])
    on = d * lax.rsqrt(var + GN_EPS) * gng_ref[...] + gnb_ref[...]
    xo = (on + bonus_ref[...]) * g_ref[...]
    yb = jnp.dot(xo.astype(BF16), wro_ref[...], preferred_element_type=F32)
    m = yag_ref[...] + gateb_ref[...] * yb
    h_ref[...] = x_ref[...] + jnp.dot(m.astype(BF16), wo_ref[...], preferred_element_type=F32)


def _merge(o, bonus, g, gng, gnb, avg_bd, wro, yag, gates, wo, x2d, tm):
    n, d = x2d.shape
    w = RWKV_WIDTH
    row_w = pl.BlockSpec((tm, w), lambda i: (i, 0))
    row_d = pl.BlockSpec((tm, d), lambda i: (i, 0))
    return pl.pallas_call(
        _merge_body,
        grid=(n // tm,),
        in_specs=[row_w, row_w, row_w, _full(gng.shape), _full(gnb.shape), _full(avg_bd.shape), _full(wro.shape),
                  row_d, pl.BlockSpec((tm, d), lambda i: (i, 1)), _full(wo.shape), row_d],
        out_specs=row_d,
        out_shape=jax.ShapeDtypeStruct((n, d), F32),
        compiler_params=_params(("parallel",)),
    )(o, bonus, g, gng, gnb, avg_bd, wro, yag, gates, wo, x2d)


N_RANK = PEER_TOPK + 1
N_RANK_PAD = -(-N_RANK // (2 * SUBLANES)) * (2 * SUBLANES)


def _candidate_pairs():
    return [(i, j) for i in range(N_RANK) for j in range(N_RANK) if (i + 1) * (j + 1) <= N_RANK]


N_CAND = len(_candidate_pairs())
N_CAND_PAD = -(-N_CAND // (2 * SUBLANES)) * (2 * SUBLANES)


def _peer_body(h_ref, gffn_ref, wqt_ref, keys_ref, sela_ref, selb_ref, u_ref, vt_ref, gfin_ref, out_ref,
               xnt_ref, qt_ref, th_ref, e1_ref, s2_ref, e2_ref, top_ref, act_ref, c_ref, yt_ref,
               *, tb, nch):
    s = pl.program_id(1)
    ncol = tb // LANES
    neg_inf = -1e30

    def top_values(x):
        vals = []
        cur = x
        for _ in range(N_RANK):
            m = jnp.max(cur, axis=0, keepdims=True)
            vals.append(m)
            cur = jnp.where(cur == m, neg_inf, cur)
        return vals

    def prep():
        xnt = _rmsnorm(h_ref[...], gffn_ref[...]).T.astype(BF16)
        xnt_ref[...] = xnt
        qt_ref[...] = jnp.dot(wqt_ref[...], xnt, preferred_element_type=F32).astype(BF16)
        yt_ref[...] = jnp.zeros_like(yt_ref)
        top_ref[...] = jnp.zeros_like(top_ref)

        def per_head(hh, carry):
            base = pl.multiple_of(hh * (2 * PEER_DHALF), 2 * PEER_DHALF)
            s1 = jnp.dot(keys_ref[hh, 0], qt_ref[pl.ds(base, PEER_DHALF), :], preferred_element_type=F32)
            s2 = jnp.dot(keys_ref[hh, 1], qt_ref[pl.ds(base + PEER_DHALF, PEER_DHALF), :],
                         preferred_element_type=F32)
            for lc in range(ncol):
                ls = slice(lc * LANES, (lc + 1) * LANES)
                s1c = s1[:, ls]
                s2c = s2[:, ls]
                va = top_values(s1c)
                vb = top_values(s2c)
                for i in range(N_RANK):
                    top_ref[0, i:i + 1, :] = va[i]
                    top_ref[1, i:i + 1, :] = vb[i]
                row = lax.broadcasted_iota(jnp.int32, (N_CAND_PAD, LANES), 0)
                asel = _dot_w3(sela_ref[...], top_ref[0])
                bsel = _dot_w3(selb_ref[...], top_ref[1])
                cand = jnp.where(row < N_CAND, asel + bsel, neg_inf)
                best = top_values(cand)
                tau = 0.5 * (best[PEER_TOPK - 1] + best[PEER_TOPK])
                keepc = (bsel >= tau - asel) & (row < N_CAND)
                zsum = jnp.sum(jnp.where(keepc, jnp.exp(asel - va[0]) * jnp.exp(bsel - vb[0]), 0.0),
                               axis=0, keepdims=True)
                grouped = (PEER_NKEYS // SUBLANES, SUBLANES, LANES)
                th_ref[hh, lc] = (tau - s1c).reshape(grouped)
                e1_ref[hh, lc] = jnp.exp(s1c - va[0]).reshape(grouped)
                s2_ref[hh, lc] = s2c
                e2_ref[hh, lc] = jnp.exp(s2c - vb[0]) / zsum
            return carry

        lax.fori_loop(0, PEER_HEADS, per_head, 0)

    @pl.when(s == 0)
    def _():
        prep()

    a_full = jnp.dot(u_ref[...], xnt_ref[...], preferred_element_type=F32)
    for cc in range(ncol):
        act_ref[cc] = a_full[:, cc * LANES:(cc + 1) * LANES]

    def per_col(lc, carry):
        for i1 in range(SUBLANES):
            rows = slice(i1 * PEER_NKEYS, (i1 + 1) * PEER_NKEYS)
            acc = jnp.zeros((PEER_NKEYS, LANES), F32)
            for hh in range(PEER_HEADS):
                keep = s2_ref[hh, lc] >= th_ref[hh, lc, s, i1:i1 + 1, :]
                acc = jnp.where(keep, acc + e1_ref[hh, lc, s, i1:i1 + 1, :] * e2_ref[hh, lc], acc)
            a = act_ref[lc, rows, :]
            gelu = 0.5 * a * (1.0 + lax.erf(a * math.sqrt(0.5)))
            c_ref[lc, rows, :] = (acc * gelu).astype(BF16)
        return carry

    lax.fori_loop(0, ncol, per_col, 0)
    coef = jnp.concatenate([c_ref[cc] for cc in range(ncol)], axis=1)
    yt_ref[...] += jnp.dot(vt_ref[...], coef, preferred_element_type=F32)

    @pl.when(s == nch - 1)
    def _():
        hh = h_ref[...] + yt_ref[...].T
        out_ref[...] = _rmsnorm(hh, gfin_ref[...])


def _peer(h1, gffn, wqt, keys, sela, selb, u, vt, gfin, tb):
    n, d = h1.shape
    ne = u.shape[0]
    ec = SUBLANES * PEER_NKEYS
    nch = ne // ec
    assert ne % ec == 0
    nq = wqt.shape[0]
    ncol = tb // LANES
    key_tiles = (PEER_HEADS, ncol, PEER_NKEYS // SUBLANES, SUBLANES, LANES)
    return pl.pallas_call(
        functools.partial(_peer_body, tb=tb, nch=nch),
        grid=(n // tb, nch),
        in_specs=[
            pl.BlockSpec((tb, d), lambda i, s: (i, 0)),
            _full(gffn.shape), _full(wqt.shape), _full(keys.shape), _full(sela.shape), _full(selb.shape),
            pl.BlockSpec((ec, d), lambda i, s: (s, 0)),
            pl.BlockSpec((d, ec), lambda i, s: (0, s)),
            _full(gfin.shape),
        ],
        out_specs=pl.BlockSpec((tb, d), lambda i, s: (i, 0)),
        out_shape=jax.ShapeDtypeStruct((n, d), F32),
        scratch_shapes=[
            pltpu.VMEM((d, tb), BF16),
            pltpu.VMEM((nq, tb), BF16),
            pltpu.VMEM(key_tiles, F32),
            pltpu.VMEM(key_tiles, F32),
            pltpu.VMEM((PEER_HEADS, ncol, PEER_NKEYS, LANES), F32),
            pltpu.VMEM((PEER_HEADS, ncol, PEER_NKEYS, LANES), F32),
            pltpu.VMEM((2, N_RANK_PAD, LANES), F32),
            pltpu.VMEM((ncol, ec, LANES), F32),
            pltpu.VMEM((ncol, ec, LANES), BF16),
            pltpu.VMEM((d, tb), F32),
        ],
        compiler_params=_params(("parallel", "arbitrary")),
    )(h1, gffn, wqt, keys, sela, selb, u, vt, gfin)


def _block_diag_const(width, block, value):
    idx = jnp.arange(width) // block
    return jnp.where(idx[:, None] == idx[None, :], value, 0.0).astype(BF16)


def _tile(n, pref):
    t = min(n, pref)
    assert n % t == 0, (n, pref)
    return t


def kernel(x, meta_tokens, g_mix, w_in, conv_w, conv_b, conv_ln_g, conv_ln_b, w_conv_out, mu_shift, w0, w_up, a0, a_up, g_up, k_k, k_a, r_k, gn_g, gn_b, w_rwkv_out, w_o, g_ffn, w_q, sub_keys, expert_u, expert_v, g_final):
    depth = g_mix.shape[0]
    assert depth == 1
    b, t, d = x.shape
    n = b * t
    n_meta = meta_tokens.shape[0]
    assert n_meta <= min(HALO, CHUNK) and n_meta % SUBLANES == 0
    w = RWKV_WIDTH
    row = lambda p: p.reshape(1, -1).astype(F32)

    w_in0 = w_in[0]
    wc = w_in0[:, :2 * CONV_CH].astype(BF16)
    wr = w_in0[:, 2 * CONV_CH:2 * CONV_CH + SHIFT_TOTAL].astype(BF16)
    wg = w_in0[:, 2 * CONV_CH + SHIFT_TOTAL:].astype(BF16)
    g_mix0 = row(g_mix[0])
    zeros_w = jnp.zeros((RANK_W, w), F32)
    wup_pad = jnp.concatenate([w_up[0], jnp.zeros((RANK_A, w), F32)], axis=0).astype(BF16)
    aup_pad = jnp.concatenate([zeros_w, a_up[0]], axis=0).astype(BF16)
    ones_bd = _block_diag_const(w, RWKV_HEAD, 1.0)
    avg_bd = _block_diag_const(w, RWKV_HEAD, 1.0 / RWKV_HEAD)
    cw = jnp.concatenate([conv_w[0], jnp.zeros((1, CONV_CH), F32)], axis=0)

    x2d = x.reshape(n, d)
    u, zr, gates = _inproj(x2d, g_mix0, wc, wr, wg, _tile(n, 256))
    u_m, zr_m, _ = _inproj(meta_tokens.astype(F32), g_mix0, wc, wr, wg, n_meta)

    halo0 = jnp.concatenate([jnp.zeros((HALO - n_meta, CONV_CH), F32), u_m], axis=0)
    tt = _tile(t, 256)
    yag = _conv_branch(u.reshape(b, t, CONV_CH), halo0, cw, row(conv_b[0]), row(conv_ln_g[0]), row(conv_ln_b[0]),
                       w_conv_out[0].astype(BF16), gates.reshape(b, t, 2 * d), tt)

    tri = jnp.tril(jnp.ones((CHUNK, CHUNK), F32))
    prep_consts = (row(mu_shift[0]), row(w0[0]), wup_pad, row(a0[0]), aup_pad, g_up[0].astype(BF16),
                   row(k_k[0]), row(k_a[0]), row(r_k[0]), ones_bd)

    def rwkv(z, prev0, s0, tt_):
        tri_bd = jnp.kron(jnp.eye(tt_ // CHUNK, dtype=F32), tri).astype(BF16)
        outs = _rwkv_prep(z, prev0, *prep_consts, tri_bd, tt_)
        at, bt, kt, rt, bh, kh, v, gl, g, bonus = outs
        rh, o0, p, q = _chunk_transforms(at, bt, kt, rt, bh, kh, v, gl)
        o, sfin = _state_scan(rh, o0, p, q, s0)
        return o, g, bonus, sfin

    z_meta = jnp.concatenate([jnp.zeros((CHUNK - n_meta, SHIFT_TOTAL), F32), zr_m], axis=0)[None]
    s_zero = jnp.zeros((w // PAIR_LANES, PAIR_LANES, PAIR_LANES), F32)
    _, _, _, s_meta = rwkv(z_meta, jnp.zeros((1, SHIFT_TOTAL), F32), s_zero, CHUNK)
    o, g, bonus, _ = rwkv(zr.reshape(b, t, SHIFT_TOTAL), zr_m[n_meta - 1:n_meta], s_meta[0], tt)

    h1 = _merge(o.reshape(n, w), bonus.reshape(n, w), g.reshape(n, w), row(gn_g[0]), row(gn_b[0]), avg_bd,
                w_rwkv_out[0].astype(BF16), yag.reshape(n, d), gates, w_o[0].astype(BF16), x2d, _tile(n, 256))

    pairs = _candidate_pairs()
    sela = jnp.zeros((N_CAND_PAD, N_RANK_PAD), F32).at[jnp.arange(N_CAND), jnp.array([p[0] for p in pairs])].set(1.0)
    selb = jnp.zeros((N_CAND_PAD, N_RANK_PAD), F32).at[jnp.arange(N_CAND), jnp.array([p[1] for p in pairs])].set(1.0)
    out = _peer(h1, row(g_ffn[0]), w_q[0].T.astype(BF16), sub_keys[0].astype(BF16),
                sela.astype(BF16), selb.astype(BF16), expert_u[0].astype(BF16), expert_v[0].T.astype(BF16),
                row(g_final), _tile(n, 512))
    return out.reshape(b, t, d)
```

```python
import functools
import math

import jax
import jax.numpy as jnp
from jax import lax
from jax.experimental import pallas as pl
from jax.experimental.pallas import tpu as pltpu

F32 = jnp.float32
BF16 = jnp.bfloat16

CONV_CH = 512
CONV_KERNEL = 31
RWKV_WIDTH = 512
RWKV_HEAD = 64
RWKV_HEADS = RWKV_WIDTH // RWKV_HEAD
RANK_W = 64
RANK_A = 64
RANK_G = 128
SHIFT_TOTAL = 3 * RWKV_WIDTH + RANK_W + RANK_A + RANK_G
PEER_HEADS = 8
PEER_NKEYS = 128
PEER_DHALF = 128
PEER_TOPK = 16
RMS_EPS = 1e-6
LN_EPS = 1e-5
GN_EPS = 64e-5

LANES = 128
SUBLANES = 8
VMEM_LIMIT_BYTES = 56 * 1024 * 1024

CHUNK = 64
HALO = 32
GROUP_LANES = 256
PAIR_LANES = 128

NT_DIMS = (((1,), (1,)), ((), ()))
TN_DIMS = (((0,), (0,)), ((), ()))


def _sigmoid(x):
    return 1.0 / (1.0 + jnp.exp(-x))


def _dot(a, b):
    return jnp.dot(a.astype(BF16), b.astype(BF16), preferred_element_type=F32)


def _dot_nt(a, b):
    return lax.dot_general(a.astype(BF16), b.astype(BF16), NT_DIMS, preferred_element_type=F32)


def _dot_tn(a, b):
    return lax.dot_general(a.astype(BF16), b.astype(BF16), TN_DIMS, preferred_element_type=F32)


def _split3(x):
    hi = x.astype(BF16)
    r1 = x - hi.astype(F32)
    mid = r1.astype(BF16)
    lo = (r1 - mid.astype(F32)).astype(BF16)
    return hi, mid, lo


def _dot_x2(x, w_bf16):
    hi = x.astype(BF16)
    lo = (x - hi.astype(F32)).astype(BF16)
    return jnp.dot(hi, w_bf16, preferred_element_type=F32) + jnp.dot(lo, w_bf16, preferred_element_type=F32)


def _dot_w3(w_bf16, x):
    hi, mid, lo = _split3(x)
    acc = jnp.dot(w_bf16, hi, preferred_element_type=F32)
    acc = acc + jnp.dot(w_bf16, mid, preferred_element_type=F32)
    return acc + jnp.dot(w_bf16, lo, preferred_element_type=F32)


def _rmsnorm(x, g):
    ms = jnp.mean(x * x, axis=-1, keepdims=True)
    return x * lax.rsqrt(ms + RMS_EPS) * g


def _params(semantics):
    return pltpu.CompilerParams(dimension_semantics=semantics, vmem_limit_bytes=VMEM_LIMIT_BYTES)


def _full(shape):
    nd = len(shape)
    return pl.BlockSpec(shape, lambda *_: (0,) * nd)


def _inproj_body(x_ref, g_ref, wc_ref, wr_ref, wg_ref, u_ref, zr_ref, gate_ref):
    xn = _rmsnorm(x_ref[...], g_ref[...]).astype(BF16)
    zc = jnp.dot(xn, wc_ref[...], preferred_element_type=F32)
    u_ref[...] = zc[:, :CONV_CH] * _sigmoid(zc[:, CONV_CH:])
    zr_ref[...] = jnp.dot(xn, wr_ref[...], preferred_element_type=F32)
    gate_ref[...] = _sigmoid(jnp.dot(xn, wg_ref[...], preferred_element_type=F32))


def _inproj(x2d, g, wc, wr, wg, tm):
    n, d = x2d.shape
    ng = wg.shape[1]
    return pl.pallas_call(
        _inproj_body,
        grid=(n // tm,),
        in_specs=[
            pl.BlockSpec((tm, d), lambda i: (i, 0)),
            _full(g.shape), _full(wc.shape), _full(wr.shape), _full(wg.shape),
        ],
        out_specs=[
            pl.BlockSpec((tm, CONV_CH), lambda i: (i, 0)),
            pl.BlockSpec((tm, SHIFT_TOTAL), lambda i: (i, 0)),
            pl.BlockSpec((tm, ng), lambda i: (i, 0)),
        ],
        out_shape=[
            jax.ShapeDtypeStruct((n, CONV_CH), F32),
            jax.ShapeDtypeStruct((n, SHIFT_TOTAL), F32),
            jax.ShapeDtypeStruct((n, ng), F32),
        ],
        compiler_params=_params(("parallel",)),
    )(x2d, g, wc, wr, wg)


CONV_ROWS = 32


def _conv_body(u_ref, halo0_ref, cw_ref, cb_ref, lng_ref, lnb_ref, wo_ref, gate_ref, out_ref,
               ext_ref, sh_ref, y_ref, *, tt):
    t = pl.program_id(1)

    @pl.when(t == 0)
    def _():
        ext_ref[0:HALO, :] = halo0_ref[...]

    @pl.when(t > 0)
    def _():
        ext_ref[0:HALO, :] = ext_ref[tt:tt + HALO, :]

    ext_ref[HALO:HALO + tt, :] = u_ref[0]
    first = HALO - (CONV_KERNEL - 1)
    last_a = [(CONV_KERNEL - 1 - b) // SUBLANES for b in range(SUBLANES)]
    for b in range(SUBLANES):
        nrow = tt + SUBLANES * last_a[b]
        sh_ref[b, 0:nrow, :] = ext_ref[first + b:first + b + nrow, :]
    for r0 in range(0, tt, CONV_ROWS):
        acc = jnp.broadcast_to(cb_ref[...], (CONV_ROWS, CONV_CH))
        for b in range(SUBLANES):
            for a in range(last_a[b] + 1):
                j = SUBLANES * a + b
                acc = acc + sh_ref[b, r0 + SUBLANES * a:r0 + SUBLANES * a + CONV_ROWS, :] * cw_ref[j:j + 1, :]
        y_ref[r0:r0 + CONV_ROWS, :] = acc
    y = y_ref[...]
    mean = jnp.mean(y, axis=-1, keepdims=True)
    d = y - mean
    var = jnp.mean(d * d, axis=-1, keepdims=True)
    yn = d * lax.rsqrt(var + LN_EPS) * lng_ref[...] + lnb_ref[...]
    act = yn * _sigmoid(yn)
    ya = jnp.dot(act.astype(BF16), wo_ref[...], preferred_element_type=F32)
    out_ref[0] = ya * gate_ref[0]


def _conv_branch(u, halo0, cw, cb, lng, lnb, wo, gates, tt):
    b, t, _ = u.shape
    d = wo.shape[1]
    return pl.pallas_call(
        functools.partial(_conv_body, tt=tt),
        grid=(b, t // tt),
        in_specs=[
            pl.BlockSpec((1, tt, CONV_CH), lambda i, j: (i, j, 0)),
            _full(halo0.shape), _full(cw.shape), _full(cb.shape), _full(lng.shape), _full(lnb.shape),
            _full(wo.shape),
            pl.BlockSpec((1, tt, d), lambda i, j: (i, j, 0)),
        ],
        out_specs=pl.BlockSpec((1, tt, d), lambda i, j: (i, j, 0)),
        out_shape=jax.ShapeDtypeStruct((b, t, d), F32),
        scratch_shapes=[pltpu.VMEM((tt + HALO, CONV_CH), F32),
                        pltpu.VMEM((SUBLANES, tt + HALO, CONV_CH), F32),
                        pltpu.VMEM((tt, CONV_CH), F32)],
        compiler_params=_params(("parallel", "arbitrary")),
    )(u, halo0, cw, cb, lng, lnb, wo, gates)


def _rwkv_prep_body(z_ref, prev0_ref, mu_ref, w0_ref, wup_ref, a0_ref, aup_ref, gup_ref,
                    kk_ref, ka_ref, rk_ref, ones_ref, tri_ref,
                    at_ref, bt_ref, kt_ref, rt_ref, bh_ref, kh_ref, v_ref, gl_ref, g_ref, bonus_ref,
                    ext_ref, *, tt):
    t = pl.program_id(1)
    w = RWKV_WIDTH

    @pl.when(t == 0)
    def _():
        ext_ref[SUBLANES - 1:SUBLANES, :] = prev0_ref[...]

    @pl.when(t > 0)
    def _():
        ext_ref[SUBLANES - 1:SUBLANES, :] = ext_ref[SUBLANES + tt - 1:SUBLANES + tt, :]

    z = z_ref[0]
    ext_ref[SUBLANES:SUBLANES + tt, :] = z
    zprev = ext_ref[SUBLANES - 1:SUBLANES - 1 + tt, :]
    zs = z + (zprev - z) * mu_ref[...]
    r = zs[:, 0:w]
    k = zs[:, w:2 * w]
    v = zs[:, 2 * w:3 * w]
    lwla = zs[:, 3 * w:3 * w + RANK_W + RANK_A]
    lg = zs[:, 3 * w + RANK_W + RANK_A:]
    wlin = w0_ref[...] + jnp.dot(jnp.tanh(lwla).astype(BF16), wup_ref[...], preferred_element_type=F32)
    x = -wlin
    softplus = jnp.maximum(x, 0.0) + jnp.log(1.0 + jnp.exp(-jnp.abs(x)))
    wlog = -softplus - 0.5
    logdecay = -jnp.exp(wlog)
    a = _sigmoid(a0_ref[...] + jnp.dot(lwla.astype(BF16), aup_ref[...], preferred_element_type=F32))
    g = jnp.dot(_sigmoid(lg).astype(BF16), gup_ref[...], preferred_element_type=F32)
    kk = k * kk_ref[...]
    ss = _dot_x2(kk * kk, ones_ref[...])
    kk = kk * lax.rsqrt(jnp.maximum(ss, 1e-24))
    k2 = k * (1.0 + (a - 1.0) * ka_ref[...])
    bonus_ref[0] = _dot_x2(r * k2 * rk_ref[...], ones_ref[...]) * v
    g_ref[0] = g

    cs = _dot_w3(tri_ref[...], logdecay)
    nchunk = tt // CHUNK
    cs3 = cs.reshape(nchunk, CHUNK, w)
    csl3 = cs3[:, CHUNK - 1:CHUNK, :]
    csl = jnp.broadcast_to(csl3, (nchunk, CHUNK, w)).reshape(tt, w)
    gl_ref[0] = jnp.broadcast_to(jnp.exp(csl3), (nchunk, SUBLANES, w))
    inv_g = jnp.exp(-cs)
    to_end = jnp.exp(csl - cs)
    avec = -kk
    bvec = kk * a
    at_ref[0] = avec * jnp.exp(cs - logdecay)
    bt_ref[0] = bvec * inv_g
    kt_ref[0] = k2 * inv_g
    rt_ref[0] = r * jnp.exp(cs)
    bh_ref[0] = bvec * to_end
    kh_ref[0] = k2 * to_end
    v_ref[0] = v


def _rwkv_prep(z, prev0, mu, w0, wup, a0, aup, gup, k_k, k_a, r_k, ones_bd, tri_bd, tt):
    b, t, _ = z.shape
    w = RWKV_WIDTH
    seq = pl.BlockSpec((1, tt, w), lambda i, j: (i, j, 0))
    seq_shape = jax.ShapeDtypeStruct((b, t, w), F32)
    consts = (prev0, mu, w0, wup, a0, aup, gup, k_k, k_a, r_k, ones_bd, tri_bd)
    return pl.pallas_call(
        functools.partial(_rwkv_prep_body, tt=tt),
        grid=(b, t // tt),
        in_specs=[pl.BlockSpec((1, tt, SHIFT_TOTAL), lambda i, j: (i, j, 0))] + [_full(c.shape) for c in consts],
        out_specs=[seq] * 7 + [pl.BlockSpec((1, tt // CHUNK, SUBLANES, w), lambda i, j: (i, j, 0, 0)), seq, seq],
        out_shape=[seq_shape] * 7 + [jax.ShapeDtypeStruct((b, t // CHUNK, SUBLANES, w), F32), seq_shape, seq_shape],
        scratch_shapes=[pltpu.VMEM((tt + SUBLANES, SHIFT_TOTAL), F32)],
        compiler_params=_params(("parallel", "arbitrary")),
    )(z, *consts)


def _chunk_body(at_ref, bt_ref, kt_ref, rt_ref, bh_ref, kh_ref, v_ref,
                rh_ref, o0_ref, p_ref, q_ref, *, cps):
    gw = GROUP_LANES
    heads = gw // RWKV_HEAD
    rows = heads * CHUNK
    ri = lax.broadcasted_iota(jnp.int32, (rows, rows), 0)
    ci = lax.broadcasted_iota(jnp.int32, (rows, rows), 1)
    same = (ri // CHUNK) == (ci // CHUNK)
    strict = same & (ci < ri)
    incl = same & (ci <= ri)
    eye = ri == ci
    lane = lax.broadcasted_iota(jnp.int32, (1, gw), 1) // RWKV_HEAD
    lri = lax.broadcasted_iota(jnp.int32, (gw, gw), 0)
    lci = lax.broadcasted_iota(jnp.int32, (gw, gw), 1)
    lsame = (lri // RWKV_HEAD) == (lci // RWKV_HEAD)

    def stack(x):
        return jnp.concatenate([jnp.where(lane == i, x, 0.0) for i in range(heads)], axis=0)

    def unstack(x):
        out = x[0:CHUNK]
        for i in range(1, heads):
            out = out + x[i * CHUNK:(i + 1) * CHUNK]
        return out

    chains = [(ck, grp) for ck in range(cps) for grp in range(RWKV_WIDTH // gw)]

    def load(ref):
        return [ref[0, ck * CHUNK:(ck + 1) * CHUNK, grp * gw:(grp + 1) * gw] for ck, grp in chains]

    def each(fn, *lists):
        return [fn(*args) for args in zip(*lists)]

    at, bt, kt, rt, bh, kh, v = (load(r) for r in (at_ref, bt_ref, kt_ref, rt_ref, bh_ref, kh_ref, v_ref))
    xa, xr, yb, yk, vs = (each(stack, x) for x in (at, rt, bt, kt, v))
    a_ab = each(lambda x, y: jnp.where(strict, _dot_nt(x, y), 0.0), xa, yb)
    a_ak = each(lambda x, y: jnp.where(strict, _dot_nt(x, y), 0.0), xa, yk)
    a_rb = each(lambda x, y: jnp.where(incl, _dot_nt(x, y), 0.0), xr, yb)
    a_rk = each(lambda x, y: jnp.where(incl, _dot_nt(x, y), 0.0), xr, yk)
    tm = each(lambda a: jnp.where(eye, 1.0, 0.0) + a, a_ab)
    apow = a_ab
    for _ in range(int(math.log2(CHUNK)) - 1):
        apow = each(lambda a: _dot(a, a), apow)
        tm = each(lambda t, a: t + _dot(t, a), tm, apow)
    akv = each(_dot, a_ak, vs)
    tw = each(lambda t, x, y: _dot(t, jnp.concatenate([x, y], axis=1)), tm, xa, akv)
    rb = each(_dot, a_rb, tw)
    ov = each(_dot, a_rk, vs)
    rhat = each(lambda x, r: unstack(x + r[:, :gw]), xr, rb)
    o0 = each(lambda o, r: unstack(o + r[:, gw:]), ov, rb)
    ahat = each(lambda t: unstack(t[:, :gw]), tw)
    u0 = each(lambda t: unstack(t[:, gw:]), tw)
    pm = each(lambda a, b: jnp.where(lsame, _dot_tn(a, b), 0.0), ahat, bh)
    qm = each(lambda u, b, vv, k: jnp.where(lsame, _dot_tn(u, b) + _dot_tn(vv, k), 0.0), u0, bh, v, kh)
    for i, (ck, grp) in enumerate(chains):
        sl = slice(grp * gw, (grp + 1) * gw)
        tr = slice(ck * CHUNK, (ck + 1) * CHUNK)
        rh_ref[0, tr, sl] = rhat[i]
        o0_ref[0, tr, sl] = o0[i]
        for pr in range(gw // PAIR_LANES):
            ps = slice(pr * PAIR_LANES, (pr + 1) * PAIR_LANES)
            p_ref[0, ck, grp * (gw // PAIR_LANES) + pr] = pm[i][ps, ps]
            q_ref[0, ck, grp * (gw // PAIR_LANES) + pr] = qm[i][ps, ps]


def _chunk_transforms(at, bt, kt, rt, bh, kh, v):
    b, t, w = at.shape
    nc = t // CHUNK
    cps = 2 if nc % 2 == 0 else 1
    npair = w // PAIR_LANES
    seq = pl.BlockSpec((1, cps * CHUNK, w), lambda i, j: (i, j, 0))
    mat = pl.BlockSpec((1, cps, npair, PAIR_LANES, PAIR_LANES), lambda i, j: (i, j, 0, 0, 0))
    return pl.pallas_call(
        functools.partial(_chunk_body, cps=cps),
        grid=(b, nc // cps),
        in_specs=[seq] * 7,
        out_specs=[seq, seq, mat, mat],
        out_shape=[
            jax.ShapeDtypeStruct((b, t, w), F32),
            jax.ShapeDtypeStruct((b, t, w), F32),
            jax.ShapeDtypeStruct((b, nc, npair, PAIR_LANES, PAIR_LANES), F32),
            jax.ShapeDtypeStruct((b, nc, npair, PAIR_LANES, PAIR_LANES), F32),
        ],
        compiler_params=_params(("parallel", "parallel")),
    )(at, bt, kt, rt, bh, kh, v)


def _scan_body(rh_ref, o0_ref, p_ref, q_ref, gl_ref, s0_ref, o_ref, sfin_ref, s_ref):
    c = pl.program_id(0)
    nb, npair = s_ref.shape[0], s_ref.shape[1]

    @pl.when(c == 0)
    def _():
        for bi in range(nb):
            s_ref[bi] = s0_ref[...]

    for bi in range(nb):
        for pr in range(npair):
            ps = slice(pr * PAIR_LANES, (pr + 1) * PAIR_LANES)
            s = s_ref[bi, pr]
            s_hi = s.astype(BF16)
            s_lo = (s - s_hi.astype(F32)).astype(BF16)
            o_ref[bi, :, ps] = _dot_nt(rh_ref[bi, :, ps], s_hi) + o0_ref[bi, :, ps]
            p = p_ref[bi, 0, pr].astype(BF16)
            s_ref[bi, pr] = (s * gl_ref[bi, 0, 0:1, ps]
                             + jnp.dot(s_hi, p, preferred_element_type=F32)
                             + jnp.dot(s_lo, p, preferred_element_type=F32)
                             + q_ref[bi, 0, pr])

    @pl.when(c == pl.num_programs(0) - 1)
    def _():
        sfin_ref[...] = s_ref[...]


def _state_scan(rh, o0, p, q, gl, s0):
    b, t, w = rh.shape
    nc = t // CHUNK
    npair = w // PAIR_LANES
    seq = pl.BlockSpec((b, CHUNK, w), lambda c: (0, c, 0))
    mat = pl.BlockSpec((b, 1, npair, PAIR_LANES, PAIR_LANES), lambda c: (0, c, 0, 0, 0))
    return pl.pallas_call(
        _scan_body,
        grid=(nc,),
        in_specs=[seq, seq, mat, mat, pl.BlockSpec((b, 1, SUBLANES, w), lambda c: (0, c, 0, 0)), _full(s0.shape)],
        out_specs=[seq, _full((b, npair, PAIR_LANES, PAIR_LANES))],
        out_shape=[
            jax.ShapeDtypeStruct((b, t, w), F32),
            jax.ShapeDtypeStruct((b, npair, PAIR_LANES, PAIR_LANES), F32),
        ],
        scratch_shapes=[pltpu.VMEM((b, npair, PAIR_LANES, PAIR_LANES), F32)],
        compiler_params=_params(("arbitrary",)),
    )(rh, o0, p, q, gl, s0)


def _merge_body(o_ref, bonus_ref, g_ref, gng_ref, gnb_ref, avg_ref, wro_ref, yag_ref, gateb_ref, wo_ref, x_ref,
                h_ref):
    o = o_ref[...]
    mean = _dot_x2(o, avg_ref[...])
    d = o - mean
    var = _dot_x2(d * d, avg_ref[...])
    on = d * lax.rsqrt(var + GN_EPS) * gng_ref[...] + gnb_ref[...]
    xo = (on + bonus_ref[...]) * g_ref[...]
    yb = jnp.dot(xo.astype(BF16), wro_ref[...], preferred_element_type=F32)
    m = yag_ref[...] + gateb_ref[...] * yb
    h_ref[...] = x_ref[...] + jnp.dot(m.astype(BF16), wo_ref[...], preferred_element_type=F32)


def _merge(o, bonus, g, gng, gnb, avg_bd, wro, yag, gates, wo, x2d, tm):
    n, d = x2d.shape
    w = RWKV_WIDTH
    row_w = pl.BlockSpec((tm, w), lambda i: (i, 0))
    row_d = pl.BlockSpec((tm, d), lambda i: (i, 0))
    return pl.pallas_call(
        _merge_body,
        grid=(n // tm,),
        in_specs=[row_w, row_w, row_w, _full(gng.shape), _full(gnb.shape), _full(avg_bd.shape), _full(wro.shape),
                  row_d, pl.BlockSpec((tm, d), lambda i: (i, 1)), _full(wo.shape), row_d],
        out_specs=row_d,
        out_shape=jax.ShapeDtypeStruct((n, d), F32),
        compiler_params=_params(("parallel",)),
    )(o, bonus, g, gng, gnb, avg_bd, wro, yag, gates, wo, x2d)


N_RANK = PEER_TOPK + 1
N_RANK_PAD = -(-N_RANK // (2 * SUBLANES)) * (2 * SUBLANES)


def _candidate_pairs():
    return [(i, j) for i in range(N_RANK) for j in range(N_RANK) if (i + 1) * (j + 1) <= N_RANK]


N_CAND = len(_candidate_pairs())
N_CAND_PAD = -(-N_CAND // (2 * SUBLANES)) * (2 * SUBLANES)


def _peer_body(h_ref, gffn_ref, wqt_ref, keys_ref, sela_ref, selb_ref, u_ref, vt_ref, gfin_ref, out_ref,
               xnt_ref, qt_ref, th_ref, e1_ref, s2_ref, e2_ref, top_ref, act_ref, c_ref, yt_ref,
               *, tb, nch):
    s = pl.program_id(1)
    ncol = tb // LANES
    neg_inf = -1e30

    def top_values(x):
        vals = []
        cur = x
        for _ in range(N_RANK):
            m = jnp.max(cur, axis=0, keepdims=True)
            vals.append(m)
            cur = jnp.where(cur == m, neg_inf, cur)
        return vals

    def prep():
        xnt = _rmsnorm(h_ref[...], gffn_ref[...]).T.astype(BF16)
        xnt_ref[...] = xnt
        qt_ref[...] = jnp.dot(wqt_ref[...], xnt, preferred_element_type=F32).astype(BF16)
        yt_ref[...] = jnp.zeros_like(yt_ref)
        top_ref[...] = jnp.zeros_like(top_ref)

        def per_head(hh, carry):
            base = pl.multiple_of(hh * (2 * PEER_DHALF), 2 * PEER_DHALF)
            s1 = jnp.dot(keys_ref[hh, 0], qt_ref[pl.ds(base, PEER_DHALF), :], preferred_element_type=F32)
            s2 = jnp.dot(keys_ref[hh, 1], qt_ref[pl.ds(base + PEER_DHALF, PEER_DHALF), :],
                         preferred_element_type=F32)
            for lc in range(ncol):
                ls = slice(lc * LANES, (lc + 1) * LANES)
                s1c = s1[:, ls]
                s2c = s2[:, ls]
                va = top_values(s1c)
                vb = top_values(s2c)
                for i in range(N_RANK):
                    top_ref[0, i:i + 1, :] = va[i]
                    top_ref[1, i:i + 1, :] = vb[i]
                row = lax.broadcasted_iota(jnp.int32, (N_CAND_PAD, LANES), 0)
                asel = _dot_w3(sela_ref[...], top_ref[0])
                bsel = _dot_w3(selb_ref[...], top_ref[1])
                cand = jnp.where(row < N_CAND, asel + bsel, neg_inf)
                best = top_values(cand)
                tau = 0.5 * (best[PEER_TOPK - 1] + best[PEER_TOPK])
                keepc = (bsel >= tau - asel) & (row < N_CAND)
                zsum = jnp.sum(jnp.where(keepc, jnp.exp(asel - va[0]) * jnp.exp(bsel - vb[0]), 0.0),
                               axis=0, keepdims=True)
                grouped = (PEER_NKEYS // SUBLANES, SUBLANES, LANES)
                th_ref[hh, lc] = (tau - s1c).reshape(grouped)
                e1_ref[hh, lc] = jnp.exp(s1c - va[0]).reshape(grouped)
                s2_ref[hh, lc] = s2c
                e2_ref[hh, lc] = jnp.exp(s2c - vb[0]) / zsum
            return carry

        lax.fori_loop(0, PEER_HEADS, per_head, 0)

    @pl.when(s == 0)
    def _():
        prep()

    a_full = jnp.dot(u_ref[...], xnt_ref[...], preferred_element_type=F32)
    for cc in range(ncol):
        act_ref[cc] = a_full[:, cc * LANES:(cc + 1) * LANES]

    def per_col(lc, carry):
        for i1 in range(SUBLANES):
            rows = slice(i1 * PEER_NKEYS, (i1 + 1) * PEER_NKEYS)
            acc = jnp.zeros((PEER_NKEYS, LANES), F32)
            for hh in range(PEER_HEADS):
                keep = s2_ref[hh, lc] >= th_ref[hh, lc, s, i1:i1 + 1, :]
                acc = jnp.where(keep, acc + e1_ref[hh, lc, s, i1:i1 + 1, :] * e2_ref[hh, lc], acc)
            a = act_ref[lc, rows, :]
            gelu = 0.5 * a * (1.0 + lax.erf(a * math.sqrt(0.5)))
            c_ref[lc, rows, :] = (acc * gelu).astype(BF16)
        return carry

    lax.fori_loop(0, ncol, per_col, 0)
    coef = jnp.concatenate([c_ref[cc] for cc in range(ncol)], axis=1)
    yt_ref[...] += jnp.dot(vt_ref[...], coef, preferred_element_type=F32)

    @pl.when(s == nch - 1)
    def _():
        hh = h_ref[...] + yt_ref[...].T
        out_ref[...] = _rmsnorm(hh, gfin_ref[...])


def _peer(h1, gffn, wqt, keys, sela, selb, u, vt, gfin, tb):
    n, d = h1.shape
    ne = u.shape[0]
    ec = SUBLANES * PEER_NKEYS
    nch = ne // ec
    assert ne % ec == 0
    nq = wqt.shape[0]
    ncol = tb // LANES
    key_tiles = (PEER_HEADS, ncol, PEER_NKEYS // SUBLANES, SUBLANES, LANES)
    return pl.pallas_call(
        functools.partial(_peer_body, tb=tb, nch=nch),
        grid=(n // tb, nch),
        in_specs=[
            pl.BlockSpec((tb, d), lambda i, s: (i, 0)),
            _full(gffn.shape), _full(wqt.shape), _full(keys.shape), _full(sela.shape), _full(selb.shape),
            pl.BlockSpec((ec, d), lambda i, s: (s, 0)),
            pl.BlockSpec((d, ec), lambda i, s: (0, s)),
            _full(gfin.shape),
        ],
        out_specs=pl.BlockSpec((tb, d), lambda i, s: (i, 0)),
        out_shape=jax.ShapeDtypeStruct((n, d), F32),
        scratch_shapes=[
            pltpu.VMEM((d, tb), BF16),
            pltpu.VMEM((nq, tb), BF16),
            pltpu.VMEM(key_tiles, F32),
            pltpu.VMEM(key_tiles, F32),
            pltpu.VMEM((PEER_HEADS, ncol, PEER_NKEYS, LANES), F32),
            pltpu.VMEM((PEER_HEADS, ncol, PEER_NKEYS, LANES), F32),
            pltpu.VMEM((2, N_RANK_PAD, LANES), F32),
            pltpu.VMEM((ncol, ec, LANES), F32),
            pltpu.VMEM((ncol, ec, LANES), BF16),
            pltpu.VMEM((d, tb), F32),
        ],
        compiler_params=_params(("parallel", "arbitrary")),
    )(h1, gffn, wqt, keys, sela, selb, u, vt, gfin)


def _block_diag_const(width, block, value):
    idx = jnp.arange(width) // block
    return jnp.where(idx[:, None] == idx[None, :], value, 0.0).astype(BF16)


def _tile(n, pref):
    t = min(n, pref)
    assert n % t == 0, (n, pref)
    return t


def kernel(x, meta_tokens, g_mix, w_in, conv_w, conv_b, conv_ln_g, conv_ln_b, w_conv_out, mu_shift, w0, w_up, a0, a_up, g_up, k_k, k_a, r_k, gn_g, gn_b, w_rwkv_out, w_o, g_ffn, w_q, sub_keys, expert_u, expert_v, g_final):
    depth = g_mix.shape[0]
    assert depth == 1
    b, t, d = x.shape
    n = b * t
    n_meta = meta_tokens.shape[0]
    assert n_meta <= min(HALO, CHUNK) and n_meta % SUBLANES == 0
    w = RWKV_WIDTH
    row = lambda p: p.reshape(1, -1).astype(F32)

    w_in0 = w_in[0]
    wc = w_in0[:, :2 * CONV_CH].astype(BF16)
    wr = w_in0[:, 2 * CONV_CH:2 * CONV_CH + SHIFT_TOTAL].astype(BF16)
    wg = w_in0[:, 2 * CONV_CH + SHIFT_TOTAL:].astype(BF16)
    g_mix0 = row(g_mix[0])
    zeros_w = jnp.zeros((RANK_W, w), F32)
    wup_pad = jnp.concatenate([w_up[0], jnp.zeros((RANK_A, w), F32)], axis=0).astype(BF16)
    aup_pad = jnp.concatenate([zeros_w, a_up[0]], axis=0).astype(BF16)
    ones_bd = _block_diag_const(w, RWKV_HEAD, 1.0)
    avg_bd = _block_diag_const(w, RWKV_HEAD, 1.0 / RWKV_HEAD)
    cw = jnp.concatenate([conv_w[0], jnp.zeros((1, CONV_CH), F32)], axis=0)

    x2d = x.reshape(n, d)
    u, zr, gates = _inproj(x2d, g_mix0, wc, wr, wg, _tile(n, 256))
    u_m, zr_m, _ = _inproj(meta_tokens.astype(F32), g_mix0, wc, wr, wg, n_meta)

    halo0 = jnp.concatenate([jnp.zeros((HALO - n_meta, CONV_CH), F32), u_m], axis=0)
    tt = _tile(t, 256)
    yag = _conv_branch(u.reshape(b, t, CONV_CH), halo0, cw, row(conv_b[0]), row(conv_ln_g[0]), row(conv_ln_b[0]),
                       w_conv_out[0].astype(BF16), gates.reshape(b, t, 2 * d), tt)

    tri = jnp.tril(jnp.ones((CHUNK, CHUNK), F32))
    prep_consts = (row(mu_shift[0]), row(w0[0]), wup_pad, row(a0[0]), aup_pad, g_up[0].astype(BF16),
                   row(k_k[0]), row(k_a[0]), row(r_k[0]), ones_bd)

    def rwkv(z, prev0, s0, tt_):
        tri_bd = jnp.kron(jnp.eye(tt_ // CHUNK, dtype=F32), tri).astype(BF16)
        outs = _rwkv_prep(z, prev0, *prep_consts, tri_bd, tt_)
        at, bt, kt, rt, bh, kh, v, gl, g, bonus = outs
        rh, o0, p, q = _chunk_transforms(at, bt, kt, rt, bh, kh, v)
        o, sfin = _state_scan(rh, o0, p, q, gl, s0)
        return o, g, bonus, sfin

    z_meta = jnp.concatenate([jnp.zeros((CHUNK - n_meta, SHIFT_TOTAL), F32), zr_m], axis=0)[None]
    s_zero = jnp.zeros((w // PAIR_LANES, PAIR_LANES, PAIR_LANES), F32)
    _, _, _, s_meta = rwkv(z_meta, jnp.zeros((1, SHIFT_TOTAL), F32), s_zero, CHUNK)
    o, g, bonus, _ = rwkv(zr.reshape(b, t, SHIFT_TOTAL), zr_m[n_meta - 1:n_meta], s_meta[0], tt)

    h1 = _merge(o.reshape(n, w), bonus.reshape(n, w), g.reshape(n, w), row(gn_g[0]), row(gn_b[0]), avg_bd,
                w_rwkv_out[0].astype(BF16), yag.reshape(n, d), gates, w_o[0].astype(BF16), x2d, _tile(n, 256))

    pairs = _candidate_pairs()
    sela = jnp.zeros((N_CAND_PAD, N_RANK_PAD), F32).at[jnp.arange(N_CAND), jnp.array([p[0] for p in pairs])].set(1.0)
    selb = jnp.zeros((N_CAND_PAD, N_RANK_PAD), F32).at[jnp.arange(N_CAND), jnp.array([p[1] for p in pairs])].set(1.0)
    out = _peer(h1, row(g_ffn[0]), w_q[0].T.astype(BF16), sub_keys[0].astype(BF16),
                sela.astype(BF16), selb.astype(BF16), expert_u[0].astype(BF16), expert_v[0].T.astype(BF16),
                row(g_final), _tile(n, 512))
    return out.reshape(b, t, d)
```

```python
import functools
import math

import jax
import jax.numpy as jnp
from jax import lax
from jax.experimental import pallas as pl
from jax.experimental.pallas import tpu as pltpu

F32 = jnp.float32
BF16 = jnp.bfloat16

CONV_CH = 512
CONV_KERNEL = 31
RWKV_WIDTH = 512
RWKV_HEAD = 64
RWKV_HEADS = RWKV_WIDTH // RWKV_HEAD
RANK_W = 64
RANK_A = 64
RANK_G = 128
SHIFT_TOTAL = 3 * RWKV_WIDTH + RANK_W + RANK_A + RANK_G
PEER_HEADS = 8
PEER_NKEYS = 128
PEER_DHALF = 128
PEER_TOPK = 16
RMS_EPS = 1e-6
LN_EPS = 1e-5
GN_EPS = 64e-5

LANES = 128
SUBLANES = 8
VMEM_LIMIT_BYTES = 58 * 1024 * 1024

CHUNK = 64
HALO = 32
GROUP_LANES = 256
PAIR_LANES = 128

NT_DIMS = (((1,), (1,)), ((), ()))
TN_DIMS = (((0,), (0,)), ((), ()))


def _sigmoid(x):
    return 1.0 / (1.0 + jnp.exp(-x))


def _dot(a, b):
    return jnp.dot(a.astype(BF16), b.astype(BF16), preferred_element_type=F32)


def _dot_nt(a, b):
    return lax.dot_general(a.astype(BF16), b.astype(BF16), NT_DIMS, preferred_element_type=F32)


def _dot_tn(a, b):
    return lax.dot_general(a.astype(BF16), b.astype(BF16), TN_DIMS, preferred_element_type=F32)


def _split3(x):
    hi = x.astype(BF16)
    r1 = x - hi.astype(F32)
    mid = r1.astype(BF16)
    lo = (r1 - mid.astype(F32)).astype(BF16)
    return hi, mid, lo


def _dot_x2(x, w_bf16):
    hi = x.astype(BF16)
    lo = (x - hi.astype(F32)).astype(BF16)
    return jnp.dot(hi, w_bf16, preferred_element_type=F32) + jnp.dot(lo, w_bf16, preferred_element_type=F32)


def _dot_w3(w_bf16, x):
    hi, mid, lo = _split3(x)
    acc = jnp.dot(w_bf16, hi, preferred_element_type=F32)
    acc = acc + jnp.dot(w_bf16, mid, preferred_element_type=F32)
    return acc + jnp.dot(w_bf16, lo, preferred_element_type=F32)


def _rmsnorm(x, g):
    ms = jnp.mean(x * x, axis=-1, keepdims=True)
    return x * lax.rsqrt(ms + RMS_EPS) * g


def _params(semantics):
    return pltpu.CompilerParams(dimension_semantics=semantics, vmem_limit_bytes=VMEM_LIMIT_BYTES)


def _full(shape):
    nd = len(shape)
    return pl.BlockSpec(shape, lambda *_: (0,) * nd)


def _inproj_body(x_ref, g_ref, wc_ref, wr_ref, wg_ref, u_ref, zr_ref, gate_ref):
    xn = _rmsnorm(x_ref[...], g_ref[...]).astype(BF16)
    zc = jnp.dot(xn, wc_ref[...], preferred_element_type=F32)
    u_ref[...] = zc[:, :CONV_CH] * _sigmoid(zc[:, CONV_CH:])
    zr_ref[...] = jnp.dot(xn, wr_ref[...], preferred_element_type=F32)
    gate_ref[...] = _sigmoid(jnp.dot(xn, wg_ref[...], preferred_element_type=F32))


def _inproj(x2d, g, wc, wr, wg, tm):
    n, d = x2d.shape
    ng = wg.shape[1]
    return pl.pallas_call(
        _inproj_body,
        grid=(n // tm,),
        in_specs=[
            pl.BlockSpec((tm, d), lambda i: (i, 0)),
            _full(g.shape), _full(wc.shape), _full(wr.shape), _full(wg.shape),
        ],
        out_specs=[
            pl.BlockSpec((tm, CONV_CH), lambda i: (i, 0)),
            pl.BlockSpec((tm, SHIFT_TOTAL), lambda i: (i, 0)),
            pl.BlockSpec((tm, ng), lambda i: (i, 0)),
        ],
        out_shape=[
            jax.ShapeDtypeStruct((n, CONV_CH), F32),
            jax.ShapeDtypeStruct((n, SHIFT_TOTAL), F32),
            jax.ShapeDtypeStruct((n, ng), F32),
        ],
        compiler_params=_params(("parallel",)),
    )(x2d, g, wc, wr, wg)


CONV_ROWS = 32


def _conv_body(u_ref, halo0_ref, cw_ref, cb_ref, lng_ref, lnb_ref, wo_ref, gate_ref, out_ref,
               ext_ref, sh_ref, y_ref, *, tt):
    t = pl.program_id(1)

    @pl.when(t == 0)
    def _():
        ext_ref[0:HALO, :] = halo0_ref[...]

    @pl.when(t > 0)
    def _():
        ext_ref[0:HALO, :] = ext_ref[tt:tt + HALO, :]

    ext_ref[HALO:HALO + tt, :] = u_ref[0]
    first = HALO - (CONV_KERNEL - 1)
    last_a = [(CONV_KERNEL - 1 - b) // SUBLANES for b in range(SUBLANES)]
    for b in range(SUBLANES):
        nrow = tt + SUBLANES * last_a[b]
        sh_ref[b, 0:nrow, :] = ext_ref[first + b:first + b + nrow, :]
    for r0 in range(0, tt, CONV_ROWS):
        acc = jnp.broadcast_to(cb_ref[...], (CONV_ROWS, CONV_CH))
        for b in range(SUBLANES):
            for a in range(last_a[b] + 1):
                j = SUBLANES * a + b
                acc = acc + sh_ref[b, r0 + SUBLANES * a:r0 + SUBLANES * a + CONV_ROWS, :] * cw_ref[j:j + 1, :]
        y_ref[r0:r0 + CONV_ROWS, :] = acc
    y = y_ref[...]
    mean = jnp.mean(y, axis=-1, keepdims=True)
    d = y - mean
    var = jnp.mean(d * d, axis=-1, keepdims=True)
    yn = d * lax.rsqrt(var + LN_EPS) * lng_ref[...] + lnb_ref[...]
    act = yn * _sigmoid(yn)
    ya = jnp.dot(act.astype(BF16), wo_ref[...], preferred_element_type=F32)
    out_ref[0] = ya * gate_ref[0]


def _conv_branch(u, halo0, cw, cb, lng, lnb, wo, gates, tt):
    b, t, _ = u.shape
    d = wo.shape[1]
    return pl.pallas_call(
        functools.partial(_conv_body, tt=tt),
        grid=(b, t // tt),
        in_specs=[
            pl.BlockSpec((1, tt, CONV_CH), lambda i, j: (i, j, 0)),
            _full(halo0.shape), _full(cw.shape), _full(cb.shape), _full(lng.shape), _full(lnb.shape),
            _full(wo.shape),
            pl.BlockSpec((1, tt, d), lambda i, j: (i, j, 0)),
        ],
        out_specs=pl.BlockSpec((1, tt, d), lambda i, j: (i, j, 0)),
        out_shape=jax.ShapeDtypeStruct((b, t, d), F32),
        scratch_shapes=[pltpu.VMEM((tt + HALO, CONV_CH), F32),
                        pltpu.VMEM((SUBLANES, tt + HALO, CONV_CH), F32),
                        pltpu.VMEM((tt, CONV_CH), F32)],
        compiler_params=_params(("parallel", "arbitrary")),
    )(u, halo0, cw, cb, lng, lnb, wo, gates)


def _rwkv_prep_body(z_ref, prev0_ref, mu_ref, w0_ref, wup_ref, a0_ref, aup_ref, gup_ref,
                    kk_ref, ka_ref, rk_ref, ones_ref, tri_ref,
                    at_ref, bt_ref, kt_ref, rt_ref, bh_ref, kh_ref, v_ref, gl_ref, g_ref, bonus_ref,
                    ext_ref, *, tt):
    t = pl.program_id(1)
    w = RWKV_WIDTH

    @pl.when(t == 0)
    def _():
        ext_ref[SUBLANES - 1:SUBLANES, :] = prev0_ref[...]

    @pl.when(t > 0)
    def _():
        ext_ref[SUBLANES - 1:SUBLANES, :] = ext_ref[SUBLANES + tt - 1:SUBLANES + tt, :]

    z = z_ref[0]
    ext_ref[SUBLANES:SUBLANES + tt, :] = z
    zprev = ext_ref[SUBLANES - 1:SUBLANES - 1 + tt, :]
    zs = z + (zprev - z) * mu_ref[...]
    r = zs[:, 0:w]
    k = zs[:, w:2 * w]
    v = zs[:, 2 * w:3 * w]
    lwla = zs[:, 3 * w:3 * w + RANK_W + RANK_A]
    lg = zs[:, 3 * w + RANK_W + RANK_A:]
    wlin = w0_ref[...] + jnp.dot(jnp.tanh(lwla).astype(BF16), wup_ref[...], preferred_element_type=F32)
    x = -wlin
    softplus = jnp.maximum(x, 0.0) + jnp.log(1.0 + jnp.exp(-jnp.abs(x)))
    wlog = -softplus - 0.5
    logdecay = -jnp.exp(wlog)
    a = _sigmoid(a0_ref[...] + jnp.dot(lwla.astype(BF16), aup_ref[...], preferred_element_type=F32))
    g = jnp.dot(_sigmoid(lg).astype(BF16), gup_ref[...], preferred_element_type=F32)
    kk = k * kk_ref[...]
    ss = _dot_x2(kk * kk, ones_ref[...])
    kk = kk * lax.rsqrt(jnp.maximum(ss, 1e-24))
    k2 = k * (1.0 + (a - 1.0) * ka_ref[...])
    bonus_ref[0] = _dot_x2(r * k2 * rk_ref[...], ones_ref[...]) * v
    g_ref[0] = g

    cs = _dot_w3(tri_ref[...], logdecay)
    nchunk = tt // CHUNK
    cs3 = cs.reshape(nchunk, CHUNK, w)
    csl3 = cs3[:, CHUNK - 1:CHUNK, :]
    csl = jnp.broadcast_to(csl3, (nchunk, CHUNK, w)).reshape(tt, w)
    gl_ref[0] = jnp.broadcast_to(jnp.exp(csl3), (nchunk, SUBLANES, w))
    inv_g = jnp.exp(-cs)
    to_end = jnp.exp(csl - cs)
    avec = -kk
    bvec = kk * a
    at_ref[0] = avec * jnp.exp(cs - logdecay)
    bt_ref[0] = bvec * inv_g
    kt_ref[0] = k2 * inv_g
    rt_ref[0] = r * jnp.exp(cs)
    bh_ref[0] = bvec * to_end
    kh_ref[0] = k2 * to_end
    v_ref[0] = v


def _rwkv_prep(z, prev0, mu, w0, wup, a0, aup, gup, k_k, k_a, r_k, ones_bd, tri_bd, tt):
    b, t, _ = z.shape
    w = RWKV_WIDTH
    seq = pl.BlockSpec((1, tt, w), lambda i, j: (i, j, 0))
    seq_shape = jax.ShapeDtypeStruct((b, t, w), F32)
    consts = (prev0, mu, w0, wup, a0, aup, gup, k_k, k_a, r_k, ones_bd, tri_bd)
    return pl.pallas_call(
        functools.partial(_rwkv_prep_body, tt=tt),
        grid=(b, t // tt),
        in_specs=[pl.BlockSpec((1, tt, SHIFT_TOTAL), lambda i, j: (i, j, 0))] + [_full(c.shape) for c in consts],
        out_specs=[seq] * 7 + [pl.BlockSpec((1, tt // CHUNK, SUBLANES, w), lambda i, j: (i, j, 0, 0)), seq, seq],
        out_shape=[seq_shape] * 7 + [jax.ShapeDtypeStruct((b, t // CHUNK, SUBLANES, w), F32), seq_shape, seq_shape],
        scratch_shapes=[pltpu.VMEM((tt + SUBLANES, SHIFT_TOTAL), F32)],
        compiler_params=_params(("parallel", "arbitrary")),
    )(z, *consts)


def _chunk_body(at_ref, bt_ref, kt_ref, rt_ref, bh_ref, kh_ref, v_ref,
                rh_ref, o0_ref, p_ref, q_ref, *, cps):
    gw = GROUP_LANES
    heads = gw // RWKV_HEAD
    rows = heads * CHUNK
    ri = lax.broadcasted_iota(jnp.int32, (rows, rows), 0)
    ci = lax.broadcasted_iota(jnp.int32, (rows, rows), 1)
    same = (ri // CHUNK) == (ci // CHUNK)
    strict = same & (ci < ri)
    incl = same & (ci <= ri)
    eye = ri == ci
    lane = lax.broadcasted_iota(jnp.int32, (1, gw), 1) // RWKV_HEAD
    lri = lax.broadcasted_iota(jnp.int32, (gw, gw), 0)
    lci = lax.broadcasted_iota(jnp.int32, (gw, gw), 1)
    lsame = (lri // RWKV_HEAD) == (lci // RWKV_HEAD)

    def stack(x):
        return jnp.concatenate([jnp.where(lane == i, x, 0.0) for i in range(heads)], axis=0)

    def unstack(x):
        out = x[0:CHUNK]
        for i in range(1, heads):
            out = out + x[i * CHUNK:(i + 1) * CHUNK]
        return out

    chains = [(ck, grp) for ck in range(cps) for grp in range(RWKV_WIDTH // gw)]

    def load(ref):
        return [ref[0, ck * CHUNK:(ck + 1) * CHUNK, grp * gw:(grp + 1) * gw] for ck, grp in chains]

    def each(fn, *lists):
        return [fn(*args) for args in zip(*lists)]

    at, bt, kt, rt, bh, kh, v = (load(r) for r in (at_ref, bt_ref, kt_ref, rt_ref, bh_ref, kh_ref, v_ref))
    xa, xr, yb, yk, vs = (each(stack, x) for x in (at, rt, bt, kt, v))
    a_ab = each(lambda x, y: jnp.where(strict, _dot_nt(x, y), 0.0), xa, yb)
    a_ak = each(lambda x, y: jnp.where(strict, _dot_nt(x, y), 0.0), xa, yk)
    a_rb = each(lambda x, y: jnp.where(incl, _dot_nt(x, y), 0.0), xr, yb)
    a_rk = each(lambda x, y: jnp.where(incl, _dot_nt(x, y), 0.0), xr, yk)
    tm = each(lambda a: jnp.where(eye, 1.0, 0.0) + a, a_ab)
    apow = a_ab
    for _ in range(int(math.log2(CHUNK)) - 1):
        apow = each(lambda a: _dot(a, a), apow)
        tm = each(lambda t, a: t + _dot(t, a), tm, apow)
    akv = each(_dot, a_ak, vs)
    tw = each(lambda t, x, y: _dot(t, jnp.concatenate([x, y], axis=1)), tm, xa, akv)
    rb = each(_dot, a_rb, tw)
    ov = each(_dot, a_rk, vs)
    rhat = each(lambda x, r: unstack(x + r[:, :gw]), xr, rb)
    o0 = each(lambda o, r: unstack(o + r[:, gw:]), ov, rb)
    ahat = each(lambda t: unstack(t[:, :gw]), tw)
    u0 = each(lambda t: unstack(t[:, gw:]), tw)
    pm = each(lambda a, b: jnp.where(lsame, _dot_tn(a, b), 0.0), ahat, bh)
    qm = each(lambda u, b, vv, k: jnp.where(lsame, _dot_tn(u, b) + _dot_tn(vv, k), 0.0), u0, bh, v, kh)
    for i, (ck, grp) in enumerate(chains):
        sl = slice(grp * gw, (grp + 1) * gw)
        tr = slice(ck * CHUNK, (ck + 1) * CHUNK)
        rh_ref[0, tr, sl] = rhat[i]
        o0_ref[0, tr, sl] = o0[i]
        for pr in range(gw // PAIR_LANES):
            ps = slice(pr * PAIR_LANES, (pr + 1) * PAIR_LANES)
            p_ref[0, ck, grp * (gw // PAIR_LANES) + pr] = pm[i][ps, ps]
            q_ref[0, ck, grp * (gw // PAIR_LANES) + pr] = qm[i][ps, ps]


def _chunk_transforms(at, bt, kt, rt, bh, kh, v):
    b, t, w = at.shape
    nc = t // CHUNK
    cps = 2 if nc % 2 == 0 else 1
    npair = w // PAIR_LANES
    seq = pl.BlockSpec((1, cps * CHUNK, w), lambda i, j: (i, j, 0))
    mat = pl.BlockSpec((1, cps, npair, PAIR_LANES, PAIR_LANES), lambda i, j: (i, j, 0, 0, 0))
    return pl.pallas_call(
        functools.partial(_chunk_body, cps=cps),
        grid=(b, nc // cps),
        in_specs=[seq] * 7,
        out_specs=[seq, seq, mat, mat],
        out_shape=[
            jax.ShapeDtypeStruct((b, t, w), F32),
            jax.ShapeDtypeStruct((b, t, w), F32),
            jax.ShapeDtypeStruct((b, nc, npair, PAIR_LANES, PAIR_LANES), F32),
            jax.ShapeDtypeStruct((b, nc, npair, PAIR_LANES, PAIR_LANES), F32),
        ],
        compiler_params=_params(("parallel", "parallel")),
    )(at, bt, kt, rt, bh, kh, v)


def _scan_body(rh_ref, o0_ref, p_ref, q_ref, gl_ref, s0_ref, o_ref, sfin_ref, s_ref):
    c = pl.program_id(0)
    nb, npair = s_ref.shape[0], s_ref.shape[1]

    @pl.when(c == 0)
    def _():
        for bi in range(nb):
            s_ref[bi] = s0_ref[...]

    for bi in range(nb):
        for pr in range(npair):
            ps = slice(pr * PAIR_LANES, (pr + 1) * PAIR_LANES)
            s = s_ref[bi, pr]
            s_hi = s.astype(BF16)
            s_lo = (s - s_hi.astype(F32)).astype(BF16)
            o_ref[bi, :, ps] = _dot_nt(rh_ref[bi, :, ps], s_hi) + o0_ref[bi, :, ps]
            p = p_ref[bi, 0, pr].astype(BF16)
            s_ref[bi, pr] = (s * gl_ref[bi, 0, 0:1, ps]
                             + jnp.dot(s_hi, p, preferred_element_type=F32)
                             + jnp.dot(s_lo, p, preferred_element_type=F32)
                             + q_ref[bi, 0, pr])

    @pl.when(c == pl.num_programs(0) - 1)
    def _():
        sfin_ref[...] = s_ref[...]


def _state_scan(rh, o0, p, q, gl, s0):
    b, t, w = rh.shape
    nc = t // CHUNK
    npair = w // PAIR_LANES
    seq = pl.BlockSpec((b, CHUNK, w), lambda c: (0, c, 0))
    mat = pl.BlockSpec((b, 1, npair, PAIR_LANES, PAIR_LANES), lambda c: (0, c, 0, 0, 0))
    return pl.pallas_call(
        _scan_body,
        grid=(nc,),
        in_specs=[seq, seq, mat, mat, pl.BlockSpec((b, 1, SUBLANES, w), lambda c: (0, c, 0, 0)), _full(s0.shape)],
        out_specs=[seq, _full((b, npair, PAIR_LANES, PAIR_LANES))],
        out_shape=[
            jax.ShapeDtypeStruct((b, t, w), F32),
            jax.ShapeDtypeStruct((b, npair, PAIR_LANES, PAIR_LANES), F32),
        ],
        scratch_shapes=[pltpu.VMEM((b, npair, PAIR_LANES, PAIR_LANES), F32)],
        compiler_params=_params(("arbitrary",)),
    )(rh, o0, p, q, gl, s0)


def _merge_body(o_ref, bonus_ref, g_ref, gng_ref, gnb_ref, avg_ref, wro_ref, yag_ref, gateb_ref, wo_ref, x_ref,
                h_ref):
    o = o_ref[...]
    mean = _dot_x2(o, avg_ref[...])
    d = o - mean
    var = _dot_x2(d * d, avg_ref[...])
    on = d * lax.rsqrt(var + GN_EPS) * gng_ref[...] + gnb_ref[...]
    xo = (on + bonus_ref[...]) * g_ref[...]
    yb = jnp.dot(xo.astype(BF16), wro_ref[...], preferred_element_type=F32)
    m = yag_ref[...] + gateb_ref[...] * yb
    h_ref[...] = x_ref[...] + jnp.dot(m.astype(BF16), wo_ref[...], preferred_element_type=F32)


def _merge(o, bonus, g, gng, gnb, avg_bd, wro, yag, gates, wo, x2d, tm):
    n, d = x2d.shape
    w = RWKV_WIDTH
    row_w = pl.BlockSpec((tm, w), lambda i: (i, 0))
    row_d = pl.BlockSpec((tm, d), lambda i: (i, 0))
    return pl.pallas_call(
        _merge_body,
        grid=(n // tm,),
        in_specs=[row_w, row_w, row_w, _full(gng.shape), _full(gnb.shape), _full(avg_bd.shape), _full(wro.shape),
                  row_d, pl.BlockSpec((tm, d), lambda i: (i, 1)), _full(wo.shape), row_d],
        out_specs=row_d,
        out_shape=jax.ShapeDtypeStruct((n, d), F32),
        compiler_params=_params(("parallel",)),
    )(o, bonus, g, gng, gnb, avg_bd, wro, yag, gates, wo, x2d)


PEER_KEY_TILES = 2
N_RANK = PEER_TOPK + 1
N_RANK_PAD = -(-N_RANK // (2 * SUBLANES)) * (2 * SUBLANES)


def _candidate_pairs():
    return [(i, j) for i in range(N_RANK) for j in range(N_RANK) if (i + 1) * (j + 1) <= N_RANK]


N_CAND = len(_candidate_pairs())
N_CAND_PAD = -(-N_CAND // (2 * SUBLANES)) * (2 * SUBLANES)


def _peer_body(h_ref, gffn_ref, wqt_ref, keys_ref, sela_ref, selb_ref, u_ref, vt_ref, gfin_ref, out_ref,
               xnt_ref, qt_ref, th_ref, e1_ref, s2_ref, e2_ref, top_ref, act_ref, c_ref, yt_ref,
               *, tb, nch):
    s = pl.program_id(1)
    ncol = tb // LANES
    neg_inf = -1e30

    def top_values(x):
        vals = []
        cur = x
        for _ in range(N_RANK):
            m = jnp.max(cur, axis=0, keepdims=True)
            vals.append(m)
            cur = jnp.where(cur == m, neg_inf, cur)
        return vals

    def prep():
        xnt = _rmsnorm(h_ref[...], gffn_ref[...]).T.astype(BF16)
        xnt_ref[...] = xnt
        qt_ref[...] = jnp.dot(wqt_ref[...], xnt, preferred_element_type=F32).astype(BF16)
        yt_ref[...] = jnp.zeros_like(yt_ref)
        top_ref[...] = jnp.zeros_like(top_ref)

        def per_head(hh, carry):
            base = pl.multiple_of(hh * (2 * PEER_DHALF), 2 * PEER_DHALF)
            s1 = jnp.dot(keys_ref[hh, 0], qt_ref[pl.ds(base, PEER_DHALF), :], preferred_element_type=F32)
            s2 = jnp.dot(keys_ref[hh, 1], qt_ref[pl.ds(base + PEER_DHALF, PEER_DHALF), :],
                         preferred_element_type=F32)
            for lc in range(ncol):
                ls = slice(lc * LANES, (lc + 1) * LANES)
                s1c = s1[:, ls]
                s2c = s2[:, ls]
                va = top_values(s1c)
                vb = top_values(s2c)
                for i in range(N_RANK):
                    top_ref[0, i:i + 1, :] = va[i]
                    top_ref[1, i:i + 1, :] = vb[i]
                row = lax.broadcasted_iota(jnp.int32, (N_CAND_PAD, LANES), 0)
                asel = _dot_w3(sela_ref[...], top_ref[0])
                bsel = _dot_w3(selb_ref[...], top_ref[1])
                cand = jnp.where(row < N_CAND, asel + bsel, neg_inf)
                best = top_values(cand)
                tau = 0.5 * (best[PEER_TOPK - 1] + best[PEER_TOPK])
                keepc = (bsel >= tau - asel) & (row < N_CAND)
                zsum = jnp.sum(jnp.where(keepc, jnp.exp(asel - va[0]) * jnp.exp(bsel - vb[0]), 0.0),
                               axis=0, keepdims=True)
                grouped = (PEER_NKEYS // SUBLANES, SUBLANES, LANES)
                th_ref[hh, lc] = (tau - s1c).reshape(grouped)
                e1_ref[hh, lc] = jnp.exp(s1c - va[0]).reshape(grouped)
                s2_ref[hh, lc] = s2c
                e2_ref[hh, lc] = jnp.exp(s2c - vb[0]) / zsum
            return carry

        lax.fori_loop(0, PEER_HEADS, per_head, 0)

    @pl.when(s == 0)
    def _():
        prep()

    et = SUBLANES * PEER_NKEYS
    for kt in range(PEER_KEY_TILES):
        a_part = jnp.dot(u_ref[kt * et:(kt + 1) * et, :], xnt_ref[...], preferred_element_type=F32)
        for cc in range(ncol):
            act_ref[cc, kt * et:(kt + 1) * et, :] = a_part[:, cc * LANES:(cc + 1) * LANES]

    def per_col(lc, carry):
        for kt in range(PEER_KEY_TILES):
            tile = s * PEER_KEY_TILES + kt
            for i1 in range(SUBLANES):
                r0 = (kt * SUBLANES + i1) * PEER_NKEYS
                rows = slice(r0, r0 + PEER_NKEYS)
                acc = jnp.zeros((PEER_NKEYS, LANES), F32)
                for hh in range(PEER_HEADS):
                    keep = s2_ref[hh, lc] >= th_ref[hh, lc, tile, i1:i1 + 1, :]
                    acc = jnp.where(keep, acc + e1_ref[hh, lc, tile, i1:i1 + 1, :] * e2_ref[hh, lc], acc)
                a = act_ref[lc, rows, :]
                gelu = 0.5 * a * (1.0 + lax.erf(a * math.sqrt(0.5)))
                c_ref[lc, rows, :] = (acc * gelu).astype(BF16)
        return carry

    lax.fori_loop(0, ncol, per_col, 0)
    y_part = None
    for kt in range(PEER_KEY_TILES):
        coef = jnp.concatenate([c_ref[cc, kt * et:(kt + 1) * et, :] for cc in range(ncol)], axis=1)
        p = jnp.dot(vt_ref[:, kt * et:(kt + 1) * et], coef, preferred_element_type=F32)
        y_part = p if y_part is None else y_part + p
    yt_ref[...] += y_part

    @pl.when(s == nch - 1)
    def _():
        hh = h_ref[...] + yt_ref[...].T
        out_ref[...] = _rmsnorm(hh, gfin_ref[...])


def _peer(h1, gffn, wqt, keys, sela, selb, u, vt, gfin, tb):
    n, d = h1.shape
    ne = u.shape[0]
    ec = PEER_KEY_TILES * SUBLANES * PEER_NKEYS
    nch = ne // ec
    assert ne % ec == 0
    nq = wqt.shape[0]
    ncol = tb // LANES
    key_tiles = (PEER_HEADS, ncol, PEER_NKEYS // SUBLANES, SUBLANES, LANES)
    return pl.pallas_call(
        functools.partial(_peer_body, tb=tb, nch=nch),
        grid=(n // tb, nch),
        in_specs=[
            pl.BlockSpec((tb, d), lambda i, s: (i, 0)),
            _full(gffn.shape),
            pl.BlockSpec(wqt.shape, lambda i, s: (0, 0), pipeline_mode=pl.Buffered(1)),
            _full(keys.shape), _full(sela.shape), _full(selb.shape),
            pl.BlockSpec((ec, d), lambda i, s: (s, 0)),
            pl.BlockSpec((d, ec), lambda i, s: (0, s)),
            _full(gfin.shape),
        ],
        out_specs=pl.BlockSpec((tb, d), lambda i, s: (i, 0)),
        out_shape=jax.ShapeDtypeStruct((n, d), F32),
        scratch_shapes=[
            pltpu.VMEM((d, tb), BF16),
            pltpu.VMEM((nq, tb), BF16),
            pltpu.VMEM(key_tiles, F32),
            pltpu.VMEM(key_tiles, F32),
            pltpu.VMEM((PEER_HEADS, ncol, PEER_NKEYS, LANES), F32),
            pltpu.VMEM((PEER_HEADS, ncol, PEER_NKEYS, LANES), F32),
            pltpu.VMEM((2, N_RANK_PAD, LANES), F32),
            pltpu.VMEM((ncol, ec, LANES), F32),
            pltpu.VMEM((ncol, ec, LANES), BF16),
            pltpu.VMEM((d, tb), F32),
        ],
        compiler_params=_params(("parallel", "arbitrary")),
    )(h1, gffn, wqt, keys, sela, selb, u, vt, gfin)


def _block_diag_const(width, block, value):
    idx = jnp.arange(width) // block
    return jnp.where(idx[:, None] == idx[None, :], value, 0.0).astype(BF16)


def _tile(n, pref):
    t = min(n, pref)
    assert n % t == 0, (n, pref)
    return t


def kernel(x, meta_tokens, g_mix, w_in, conv_w, conv_b, conv_ln_g, conv_ln_b, w_conv_out, mu_shift, w0, w_up, a0, a_up, g_up, k_k, k_a, r_k, gn_g, gn_b, w_rwkv_out, w_o, g_ffn, w_q, sub_keys, expert_u, expert_v, g_final):
    depth = g_mix.shape[0]
    assert depth == 1
    b, t, d = x.shape
    n = b * t
    n_meta = meta_tokens.shape[0]
    assert n_meta <= min(HALO, CHUNK) and n_meta % SUBLANES == 0
    w = RWKV_WIDTH
    row = lambda p: p.reshape(1, -1).astype(F32)

    w_in0 = w_in[0]
    wc = w_in0[:, :2 * CONV_CH].astype(BF16)
    wr = w_in0[:, 2 * CONV_CH:2 * CONV_CH + SHIFT_TOTAL].astype(BF16)
    wg = w_in0[:, 2 * CONV_CH + SHIFT_TOTAL:].astype(BF16)
    g_mix0 = row(g_mix[0])
    zeros_w = jnp.zeros((RANK_W, w), F32)
    wup_pad = jnp.concatenate([w_up[0], jnp.zeros((RANK_A, w), F32)], axis=0).astype(BF16)
    aup_pad = jnp.concatenate([zeros_w, a_up[0]], axis=0).astype(BF16)
    ones_bd = _block_diag_const(w, RWKV_HEAD, 1.0)
    avg_bd = _block_diag_const(w, RWKV_HEAD, 1.0 / RWKV_HEAD)
    cw = jnp.concatenate([conv_w[0], jnp.zeros((1, CONV_CH), F32)], axis=0)

    x2d = x.reshape(n, d)
    u, zr, gates = _inproj(x2d, g_mix0, wc, wr, wg, _tile(n, 512))
    u_m, zr_m, _ = _inproj(meta_tokens.astype(F32), g_mix0, wc, wr, wg, n_meta)

    halo0 = jnp.concatenate([jnp.zeros((HALO - n_meta, CONV_CH), F32), u_m], axis=0)
    tt = _tile(t, 256)
    yag = _conv_branch(u.reshape(b, t, CONV_CH), halo0, cw, row(conv_b[0]), row(conv_ln_g[0]), row(conv_ln_b[0]),
                       w_conv_out[0].astype(BF16), gates.reshape(b, t, 2 * d), tt)

    tri = jnp.tril(jnp.ones((CHUNK, CHUNK), F32))
    prep_consts = (row(mu_shift[0]), row(w0[0]), wup_pad, row(a0[0]), aup_pad, g_up[0].astype(BF16),
                   row(k_k[0]), row(k_a[0]), row(r_k[0]), ones_bd)

    def rwkv(z, prev0, s0, tt_):
        tri_bd = jnp.kron(jnp.eye(tt_ // CHUNK, dtype=F32), tri).astype(BF16)
        outs = _rwkv_prep(z, prev0, *prep_consts, tri_bd, tt_)
        at, bt, kt, rt, bh, kh, v, gl, g, bonus = outs
        rh, o0, p, q = _chunk_transforms(at, bt, kt, rt, bh, kh, v)
        o, sfin = _state_scan(rh, o0, p, q, gl, s0)
        return o, g, bonus, sfin

    z_meta = jnp.concatenate([jnp.zeros((CHUNK - n_meta, SHIFT_TOTAL), F32), zr_m], axis=0)[None]
    s_zero = jnp.zeros((w // PAIR_LANES, PAIR_LANES, PAIR_LANES), F32)
    _, _, _, s_meta = rwkv(z_meta, jnp.zeros((1, SHIFT_TOTAL), F32), s_zero, CHUNK)
    o, g, bonus, _ = rwkv(zr.reshape(b, t, SHIFT_TOTAL), zr_m[n_meta - 1:n_meta], s_meta[0], tt)

    h1 = _merge(o.reshape(n, w), bonus.reshape(n, w), g.reshape(n, w), row(gn_g[0]), row(gn_b[0]), avg_bd,
                w_rwkv_out[0].astype(BF16), yag.reshape(n, d), gates, w_o[0].astype(BF16), x2d, _tile(n, 512))

    pairs = _candidate_pairs()
    sela = jnp.zeros((N_CAND_PAD, N_RANK_PAD), F32).at[jnp.arange(N_CAND), jnp.array([p[0] for p in pairs])].set(1.0)
    selb = jnp.zeros((N_CAND_PAD, N_RANK_PAD), F32).at[jnp.arange(N_CAND), jnp.array([p[1] for p in pairs])].set(1.0)
    out = _peer(h1, row(g_ffn[0]), w_q[0].T.astype(BF16), sub_keys[0].astype(BF16),
                sela.astype(BF16), selb.astype(BF16), expert_u[0].astype(BF16), expert_v[0].T.astype(BF16),
                row(g_final), _tile(n, 512))
    return out.reshape(b, t, d)
```

```python
import functools
import math

import jax
import jax.numpy as jnp
from jax import lax
from jax.experimental import pallas as pl
from jax.experimental.pallas import tpu as pltpu

F32 = jnp.float32
BF16 = jnp.bfloat16

CONV_CH = 512
CONV_KERNEL = 31
RWKV_WIDTH = 512
RWKV_HEAD = 64
RWKV_HEADS = RWKV_WIDTH // RWKV_HEAD
RANK_W = 64
RANK_A = 64
RANK_G = 128
SHIFT_TOTAL = 3 * RWKV_WIDTH + RANK_W + RANK_A + RANK_G
PEER_HEADS = 8
PEER_NKEYS = 128
PEER_DHALF = 128
PEER_TOPK = 16
RMS_EPS = 1e-6
LN_EPS = 1e-5
GN_EPS = 64e-5

LANES = 128
SUBLANES = 8
VMEM_LIMIT_BYTES = 58 * 1024 * 1024

CHUNK = 64
HALO = 32
GROUP_LANES = 256
PAIR_LANES = 128

NT_DIMS = (((1,), (1,)), ((), ()))
TN_DIMS = (((0,), (0,)), ((), ()))


def _sigmoid(x):
    return 1.0 / (1.0 + jnp.exp(-x))


def _dot(a, b):
    return jnp.dot(a.astype(BF16), b.astype(BF16), preferred_element_type=F32)


def _dot_nt(a, b):
    return lax.dot_general(a.astype(BF16), b.astype(BF16), NT_DIMS, preferred_element_type=F32)


def _dot_tn(a, b):
    return lax.dot_general(a.astype(BF16), b.astype(BF16), TN_DIMS, preferred_element_type=F32)


def _split3(x):
    hi = x.astype(BF16)
    r1 = x - hi.astype(F32)
    mid = r1.astype(BF16)
    lo = (r1 - mid.astype(F32)).astype(BF16)
    return hi, mid, lo


def _dot_x2(x, w_bf16):
    hi = x.astype(BF16)
    lo = (x - hi.astype(F32)).astype(BF16)
    return jnp.dot(hi, w_bf16, preferred_element_type=F32) + jnp.dot(lo, w_bf16, preferred_element_type=F32)


def _dot_w3(w_bf16, x):
    hi, mid, lo = _split3(x)
    acc = jnp.dot(w_bf16, hi, preferred_element_type=F32)
    acc = acc + jnp.dot(w_bf16, mid, preferred_element_type=F32)
    return acc + jnp.dot(w_bf16, lo, preferred_element_type=F32)


def _rmsnorm(x, g):
    ms = jnp.mean(x * x, axis=-1, keepdims=True)
    return x * lax.rsqrt(ms + RMS_EPS) * g


def _params(semantics):
    return pltpu.CompilerParams(dimension_semantics=semantics, vmem_limit_bytes=VMEM_LIMIT_BYTES)


def _full(shape):
    nd = len(shape)
    return pl.BlockSpec(shape, lambda *_: (0,) * nd)


def _inproj_body(x_ref, g_ref, wc_ref, wr_ref, wg_ref, u_ref, zr_ref, gate_ref):
    xn = _rmsnorm(x_ref[...], g_ref[...]).astype(BF16)
    zc = jnp.dot(xn, wc_ref[...], preferred_element_type=F32)
    u_ref[...] = zc[:, :CONV_CH] * _sigmoid(zc[:, CONV_CH:])
    zr_ref[...] = jnp.dot(xn, wr_ref[...], preferred_element_type=F32)
    gate_ref[...] = _sigmoid(jnp.dot(xn, wg_ref[...], preferred_element_type=F32))


def _inproj(x2d, g, wc, wr, wg, tm):
    n, d = x2d.shape
    ng = wg.shape[1]
    return pl.pallas_call(
        _inproj_body,
        grid=(n // tm,),
        in_specs=[
            pl.BlockSpec((tm, d), lambda i: (i, 0)),
            _full(g.shape), _full(wc.shape), _full(wr.shape), _full(wg.shape),
        ],
        out_specs=[
            pl.BlockSpec((tm, CONV_CH), lambda i: (i, 0)),
            pl.BlockSpec((tm, SHIFT_TOTAL), lambda i: (i, 0)),
            pl.BlockSpec((tm, ng), lambda i: (i, 0)),
        ],
        out_shape=[
            jax.ShapeDtypeStruct((n, CONV_CH), F32),
            jax.ShapeDtypeStruct((n, SHIFT_TOTAL), F32),
            jax.ShapeDtypeStruct((n, ng), F32),
        ],
        compiler_params=_params(("parallel",)),
    )(x2d, g, wc, wr, wg)


CONV_ROWS = 32


def _conv_body(u_ref, halo0_ref, cw_ref, cb_ref, lng_ref, lnb_ref, wo_ref, gate_ref, out_ref,
               ext_ref, sh_ref, y_ref, *, tt):
    t = pl.program_id(1)

    @pl.when(t == 0)
    def _():
        ext_ref[0:HALO, :] = halo0_ref[...]

    @pl.when(t > 0)
    def _():
        ext_ref[0:HALO, :] = ext_ref[tt:tt + HALO, :]

    ext_ref[HALO:HALO + tt, :] = u_ref[0]
    first = HALO - (CONV_KERNEL - 1)
    last_a = [(CONV_KERNEL - 1 - b) // SUBLANES for b in range(SUBLANES)]
    for b in range(SUBLANES):
        nrow = tt + SUBLANES * last_a[b]
        sh_ref[b, 0:nrow, :] = ext_ref[first + b:first + b + nrow, :]
    for r0 in range(0, tt, CONV_ROWS):
        acc = jnp.broadcast_to(cb_ref[...], (CONV_ROWS, CONV_CH))
        for b in range(SUBLANES):
            for a in range(last_a[b] + 1):
                j = SUBLANES * a + b
                acc = acc + sh_ref[b, r0 + SUBLANES * a:r0 + SUBLANES * a + CONV_ROWS, :] * cw_ref[j:j + 1, :]
        y_ref[r0:r0 + CONV_ROWS, :] = acc
    y = y_ref[...]
    mean = jnp.mean(y, axis=-1, keepdims=True)
    d = y - mean
    var = jnp.mean(d * d, axis=-1, keepdims=True)
    yn = d * lax.rsqrt(var + LN_EPS) * lng_ref[...] + lnb_ref[...]
    act = yn * _sigmoid(yn)
    ya = jnp.dot(act.astype(BF16), wo_ref[...], preferred_element_type=F32)
    out_ref[0] = ya * gate_ref[0]


def _conv_branch(u, halo0, cw, cb, lng, lnb, wo, gates, tt):
    b, t, _ = u.shape
    d = wo.shape[1]
    return pl.pallas_call(
        functools.partial(_conv_body, tt=tt),
        grid=(b, t // tt),
        in_specs=[
            pl.BlockSpec((1, tt, CONV_CH), lambda i, j: (i, j, 0)),
            _full(halo0.shape), _full(cw.shape), _full(cb.shape), _full(lng.shape), _full(lnb.shape),
            _full(wo.shape),
            pl.BlockSpec((1, tt, d), lambda i, j: (i, j, 0)),
        ],
        out_specs=pl.BlockSpec((1, tt, d), lambda i, j: (i, j, 0)),
        out_shape=jax.ShapeDtypeStruct((b, t, d), F32),
        scratch_shapes=[pltpu.VMEM((tt + HALO, CONV_CH), F32),
                        pltpu.VMEM((SUBLANES, tt + HALO, CONV_CH), F32),
                        pltpu.VMEM((tt, CONV_CH), F32)],
        compiler_params=_params(("parallel", "arbitrary")),
    )(u, halo0, cw, cb, lng, lnb, wo, gates)


def _rwkv_prep_body(z_ref, prev0_ref, mu_ref, w0_ref, wup_ref, a0_ref, aup_ref, gup_ref,
                    kk_ref, ka_ref, rk_ref, ones_ref, tri_ref,
                    at_ref, bt_ref, kt_ref, rt_ref, bh_ref, kh_ref, v_ref, gl_ref, g_ref, bonus_ref,
                    ext_ref, *, tt):
    t = pl.program_id(1)
    w = RWKV_WIDTH

    @pl.when(t == 0)
    def _():
        ext_ref[SUBLANES - 1:SUBLANES, :] = prev0_ref[...]

    @pl.when(t > 0)
    def _():
        ext_ref[SUBLANES - 1:SUBLANES, :] = ext_ref[SUBLANES + tt - 1:SUBLANES + tt, :]

    z = z_ref[0]
    ext_ref[SUBLANES:SUBLANES + tt, :] = z
    zprev = ext_ref[SUBLANES - 1:SUBLANES - 1 + tt, :]
    zs = z + (zprev - z) * mu_ref[...]
    r = zs[:, 0:w]
    k = zs[:, w:2 * w]
    v = zs[:, 2 * w:3 * w]
    lwla = zs[:, 3 * w:3 * w + RANK_W + RANK_A]
    lg = zs[:, 3 * w + RANK_W + RANK_A:]
    wlin = w0_ref[...] + jnp.dot(jnp.tanh(lwla).astype(BF16), wup_ref[...], preferred_element_type=F32)
    x = -wlin
    softplus = jnp.maximum(x, 0.0) + jnp.log(1.0 + jnp.exp(-jnp.abs(x)))
    wlog = -softplus - 0.5
    logdecay = -jnp.exp(wlog)
    a = _sigmoid(a0_ref[...] + jnp.dot(lwla.astype(BF16), aup_ref[...], preferred_element_type=F32))
    g = jnp.dot(_sigmoid(lg).astype(BF16), gup_ref[...], preferred_element_type=F32)
    kk = k * kk_ref[...]
    ss = _dot_x2(kk * kk, ones_ref[...])
    kk = kk * lax.rsqrt(jnp.maximum(ss, 1e-24))
    k2 = k * (1.0 + (a - 1.0) * ka_ref[...])
    bonus_ref[0] = _dot_x2(r * k2 * rk_ref[...], ones_ref[...]) * v
    g_ref[0] = g

    cs = _dot_w3(tri_ref[...], logdecay)
    nchunk = tt // CHUNK
    cs3 = cs.reshape(nchunk, CHUNK, w)
    csl3 = cs3[:, CHUNK - 1:CHUNK, :]
    csl = jnp.broadcast_to(csl3, (nchunk, CHUNK, w)).reshape(tt, w)
    gl_ref[0] = jnp.broadcast_to(jnp.exp(csl3), (nchunk, SUBLANES, w))
    inv_g = jnp.exp(-cs)
    to_end = jnp.exp(csl - cs)
    avec = -kk
    bvec = kk * a
    at_ref[0] = avec * jnp.exp(cs - logdecay)
    bt_ref[0] = bvec * inv_g
    kt_ref[0] = k2 * inv_g
    rt_ref[0] = r * jnp.exp(cs)
    bh_ref[0] = bvec * to_end
    kh_ref[0] = k2 * to_end
    v_ref[0] = v


def _rwkv_prep(z, prev0, mu, w0, wup, a0, aup, gup, k_k, k_a, r_k, ones_bd, tri_bd, tt):
    b, t, _ = z.shape
    w = RWKV_WIDTH
    seq = pl.BlockSpec((1, tt, w), lambda i, j: (i, j, 0))
    seq_shape = jax.ShapeDtypeStruct((b, t, w), F32)
    consts = (prev0, mu, w0, wup, a0, aup, gup, k_k, k_a, r_k, ones_bd, tri_bd)
    return pl.pallas_call(
        functools.partial(_rwkv_prep_body, tt=tt),
        grid=(b, t // tt),
        in_specs=[pl.BlockSpec((1, tt, SHIFT_TOTAL), lambda i, j: (i, j, 0))] + [_full(c.shape) for c in consts],
        out_specs=[seq] * 7 + [pl.BlockSpec((1, tt // CHUNK, SUBLANES, w), lambda i, j: (i, j, 0, 0)), seq, seq],
        out_shape=[seq_shape] * 7 + [jax.ShapeDtypeStruct((b, t // CHUNK, SUBLANES, w), F32), seq_shape, seq_shape],
        scratch_shapes=[pltpu.VMEM((tt + SUBLANES, SHIFT_TOTAL), F32)],
        compiler_params=_params(("parallel", "arbitrary")),
    )(z, *consts)


def _chunk_body(at_ref, bt_ref, kt_ref, rt_ref, bh_ref, kh_ref, v_ref,
                rh_ref, o0_ref, p_ref, q_ref, *, cps):
    gw = GROUP_LANES
    heads = gw // RWKV_HEAD
    rows = heads * CHUNK
    ri = lax.broadcasted_iota(jnp.int32, (rows, rows), 0)
    ci = lax.broadcasted_iota(jnp.int32, (rows, rows), 1)
    same = (ri // CHUNK) == (ci // CHUNK)
    strict = same & (ci < ri)
    incl = same & (ci <= ri)
    eye = ri == ci
    lane = lax.broadcasted_iota(jnp.int32, (1, gw), 1) // RWKV_HEAD
    lri = lax.broadcasted_iota(jnp.int32, (gw, gw), 0)
    lci = lax.broadcasted_iota(jnp.int32, (gw, gw), 1)
    lsame = (lri // RWKV_HEAD) == (lci // RWKV_HEAD)

    def stack(x):
        return jnp.concatenate([jnp.where(lane == i, x, 0.0) for i in range(heads)], axis=0)

    def unstack(x):
        out = x[0:CHUNK]
        for i in range(1, heads):
            out = out + x[i * CHUNK:(i + 1) * CHUNK]
        return out

    chains = [(ck, grp) for ck in range(cps) for grp in range(RWKV_WIDTH // gw)]

    def load(ref):
        return [ref[0, ck * CHUNK:(ck + 1) * CHUNK, grp * gw:(grp + 1) * gw] for ck, grp in chains]

    def each(fn, *lists):
        return [fn(*args) for args in zip(*lists)]

    at, bt, kt, rt, bh, kh, v = (load(r) for r in (at_ref, bt_ref, kt_ref, rt_ref, bh_ref, kh_ref, v_ref))
    xa, xr, yb, yk, vs = (each(stack, x) for x in (at, rt, bt, kt, v))
    a_ab = each(lambda x, y: jnp.where(strict, _dot_nt(x, y), 0.0), xa, yb)
    a_ak = each(lambda x, y: jnp.where(strict, _dot_nt(x, y), 0.0), xa, yk)
    a_rb = each(lambda x, y: jnp.where(incl, _dot_nt(x, y), 0.0), xr, yb)
    a_rk = each(lambda x, y: jnp.where(incl, _dot_nt(x, y), 0.0), xr, yk)
    tm = each(lambda a: jnp.where(eye, 1.0, 0.0) + a, a_ab)
    apow = a_ab
    for _ in range(int(math.log2(CHUNK)) - 1):
        apow = each(lambda a: _dot(a, a), apow)
        tm = each(lambda t, a: t + _dot(t, a), tm, apow)
    akv = each(_dot, a_ak, vs)
    tw = each(lambda t, x, y: _dot(t, jnp.concatenate([x, y], axis=1)), tm, xa, akv)
    rb = each(_dot, a_rb, tw)
    ov = each(_dot, a_rk, vs)
    rhat = each(lambda x, r: unstack(x + r[:, :gw]), xr, rb)
    o0 = each(lambda o, r: unstack(o + r[:, gw:]), ov, rb)
    ahat = each(lambda t: unstack(t[:, :gw]), tw)
    u0 = each(lambda t: unstack(t[:, gw:]), tw)
    pm = each(lambda a, b: jnp.where(lsame, _dot_tn(a, b), 0.0), ahat, bh)
    qm = each(lambda u, b, vv, k: jnp.where(lsame, _dot_tn(u, b) + _dot_tn(vv, k), 0.0), u0, bh, v, kh)
    for i, (ck, grp) in enumerate(chains):
        sl = slice(grp * gw, (grp + 1) * gw)
        tr = slice(ck * CHUNK, (ck + 1) * CHUNK)
        rh_ref[0, tr, sl] = rhat[i]
        o0_ref[0, tr, sl] = o0[i]
        for pr in range(gw // PAIR_LANES):
            ps = slice(pr * PAIR_LANES, (pr + 1) * PAIR_LANES)
            p_ref[0, ck, grp * (gw // PAIR_LANES) + pr] = pm[i][ps, ps]
            q_ref[0, ck, grp * (gw // PAIR_LANES) + pr] = qm[i][ps, ps]


def _chunk_transforms(at, bt, kt, rt, bh, kh, v):
    b, t, w = at.shape
    nc = t // CHUNK
    cps = 4 if nc % 4 == 0 else 1
    npair = w // PAIR_LANES
    seq = pl.BlockSpec((1, cps * CHUNK, w), lambda i, j: (i, j, 0))
    mat = pl.BlockSpec((1, cps, npair, PAIR_LANES, PAIR_LANES), lambda i, j: (i, j, 0, 0, 0))
    return pl.pallas_call(
        functools.partial(_chunk_body, cps=cps),
        grid=(b, nc // cps),
        in_specs=[seq] * 7,
        out_specs=[seq, seq, mat, mat],
        out_shape=[
            jax.ShapeDtypeStruct((b, t, w), F32),
            jax.ShapeDtypeStruct((b, t, w), F32),
            jax.ShapeDtypeStruct((b, nc, npair, PAIR_LANES, PAIR_LANES), F32),
            jax.ShapeDtypeStruct((b, nc, npair, PAIR_LANES, PAIR_LANES), F32),
        ],
        compiler_params=_params(("parallel", "parallel")),
    )(at, bt, kt, rt, bh, kh, v)


def _scan_body(rh_ref, o0_ref, p_ref, q_ref, gl_ref, s0_ref, o_ref, sfin_ref, s_ref):
    c = pl.program_id(0)
    nb, npair = s_ref.shape[0], s_ref.shape[1]

    @pl.when(c == 0)
    def _():
        for bi in range(nb):
            s_ref[bi] = s0_ref[...]

    for bi in range(nb):
        for pr in range(npair):
            ps = slice(pr * PAIR_LANES, (pr + 1) * PAIR_LANES)
            s = s_ref[bi, pr]
            s_hi = s.astype(BF16)
            s_lo = (s - s_hi.astype(F32)).astype(BF16)
            o_ref[bi, :, ps] = _dot_nt(rh_ref[bi, :, ps], s_hi) + o0_ref[bi, :, ps]
            p = p_ref[bi, 0, pr].astype(BF16)
            s_ref[bi, pr] = (s * gl_ref[bi, 0, 0:1, ps]
                             + jnp.dot(s_hi, p, preferred_element_type=F32)
                             + jnp.dot(s_lo, p, preferred_element_type=F32)
                             + q_ref[bi, 0, pr])

    @pl.when(c == pl.num_programs(0) - 1)
    def _():
        sfin_ref[...] = s_ref[...]


def _state_scan(rh, o0, p, q, gl, s0):
    b, t, w = rh.shape
    nc = t // CHUNK
    npair = w // PAIR_LANES
    seq = pl.BlockSpec((b, CHUNK, w), lambda c: (0, c, 0))
    mat = pl.BlockSpec((b, 1, npair, PAIR_LANES, PAIR_LANES), lambda c: (0, c, 0, 0, 0))
    return pl.pallas_call(
        _scan_body,
        grid=(nc,),
        in_specs=[seq, seq, mat, mat, pl.BlockSpec((b, 1, SUBLANES, w), lambda c: (0, c, 0, 0)), _full(s0.shape)],
        out_specs=[seq, _full((b, npair, PAIR_LANES, PAIR_LANES))],
        out_shape=[
            jax.ShapeDtypeStruct((b, t, w), F32),
            jax.ShapeDtypeStruct((b, npair, PAIR_LANES, PAIR_LANES), F32),
        ],
        scratch_shapes=[pltpu.VMEM((b, npair, PAIR_LANES, PAIR_LANES), F32)],
        compiler_params=_params(("arbitrary",)),
    )(rh, o0, p, q, gl, s0)


def _merge_body(o_ref, bonus_ref, g_ref, gng_ref, gnb_ref, avg_ref, wro_ref, yag_ref, gateb_ref, wo_ref, x_ref,
                h_ref):
    o = o_ref[...]
    mean = _dot_x2(o, avg_ref[...])
    d = o - mean
    var = _dot_x2(d * d, avg_ref[...])
    on = d * lax.rsqrt(var + GN_EPS) * gng_ref[...] + gnb_ref[...]
    xo = (on + bonus_ref[...]) * g_ref[...]
    yb = jnp.dot(xo.astype(BF16), wro_ref[...], preferred_element_type=F32)
    m = yag_ref[...] + gateb_ref[...] * yb
    h_ref[...] = x_ref[...] + jnp.dot(m.astype(BF16), wo_ref[...], preferred_element_type=F32)


def _merge(o, bonus, g, gng, gnb, avg_bd, wro, yag, gates, wo, x2d, tm):
    n, d = x2d.shape
    w = RWKV_WIDTH
    row_w = pl.BlockSpec((tm, w), lambda i: (i, 0))
    row_d = pl.BlockSpec((tm, d), lambda i: (i, 0))
    return pl.pallas_call(
        _merge_body,
        grid=(n // tm,),
        in_specs=[row_w, row_w, row_w, _full(gng.shape), _full(gnb.shape), _full(avg_bd.shape), _full(wro.shape),
                  row_d, pl.BlockSpec((tm, d), lambda i: (i, 1)), _full(wo.shape), row_d],
        out_specs=row_d,
        out_shape=jax.ShapeDtypeStruct((n, d), F32),
        compiler_params=_params(("parallel",)),
    )(o, bonus, g, gng, gnb, avg_bd, wro, yag, gates, wo, x2d)


PEER_KEY_TILES = 2
N_RANK = PEER_TOPK + 1
N_RANK_PAD = -(-N_RANK // (2 * SUBLANES)) * (2 * SUBLANES)


def _candidate_pairs():
    return [(i, j) for i in range(N_RANK) for j in range(N_RANK) if (i + 1) * (j + 1) <= N_RANK]


N_CAND = len(_candidate_pairs())
N_CAND_PAD = -(-N_CAND // (2 * SUBLANES)) * (2 * SUBLANES)


def _peer_body(h_ref, gffn_ref, wqt_ref, keys_ref, sela_ref, selb_ref, u_ref, vt_ref, gfin_ref, out_ref,
               xnt_ref, qt_ref, th_ref, e1_ref, s2_ref, e2_ref, top_ref, act_ref, c_ref, yt_ref,
               *, tb, nch):
    s = pl.program_id(1)
    ncol = tb // LANES
    neg_inf = -1e30

    def top_values(x):
        vals = []
        cur = x
        for _ in range(N_RANK):
            m = jnp.max(cur, axis=0, keepdims=True)
            vals.append(m)
            cur = jnp.where(cur == m, neg_inf, cur)
        return vals

    def prep():
        xnt = _rmsnorm(h_ref[...], gffn_ref[...]).T.astype(BF16)
        xnt_ref[...] = xnt
        qt_ref[...] = jnp.dot(wqt_ref[...], xnt, preferred_element_type=F32).astype(BF16)
        yt_ref[...] = jnp.zeros_like(yt_ref)
        top_ref[...] = jnp.zeros_like(top_ref)

        def per_head(hh, carry):
            base = pl.multiple_of(hh * (2 * PEER_DHALF), 2 * PEER_DHALF)
            s1 = jnp.dot(keys_ref[hh, 0], qt_ref[pl.ds(base, PEER_DHALF), :], preferred_element_type=F32)
            s2 = jnp.dot(keys_ref[hh, 1], qt_ref[pl.ds(base + PEER_DHALF, PEER_DHALF), :],
                         preferred_element_type=F32)
            for lc in range(ncol):
                ls = slice(lc * LANES, (lc + 1) * LANES)
                s1c = s1[:, ls]
                s2c = s2[:, ls]
                va = top_values(s1c)
                vb = top_values(s2c)
                for i in range(N_RANK):
                    top_ref[0, i:i + 1, :] = va[i]
                    top_ref[1, i:i + 1, :] = vb[i]
                row = lax.broadcasted_iota(jnp.int32, (N_CAND_PAD, LANES), 0)
                asel = _dot_w3(sela_ref[...], top_ref[0])
                bsel = _dot_w3(selb_ref[...], top_ref[1])
                cand = jnp.where(row < N_CAND, asel + bsel, neg_inf)
                best = top_values(cand)
                tau = 0.5 * (best[PEER_TOPK - 1] + best[PEER_TOPK])
                keepc = (bsel >= tau - asel) & (row < N_CAND)
                zsum = jnp.sum(jnp.where(keepc, jnp.exp(asel - va[0]) * jnp.exp(bsel - vb[0]), 0.0),
                               axis=0, keepdims=True)
                grouped = (PEER_NKEYS // SUBLANES, SUBLANES, LANES)
                th_ref[hh, lc] = (tau - s1c).reshape(grouped)
                e1_ref[hh, lc] = jnp.exp(s1c - va[0]).reshape(grouped)
                s2_ref[hh, lc] = s2c
                e2_ref[hh, lc] = jnp.exp(s2c - vb[0]) / zsum
            return carry

        lax.fori_loop(0, PEER_HEADS, per_head, 0)

    @pl.when(s == 0)
    def _():
        prep()

    et = SUBLANES * PEER_NKEYS
    for kt in range(PEER_KEY_TILES):
        a_part = jnp.dot(u_ref[kt * et:(kt + 1) * et, :], xnt_ref[...], preferred_element_type=F32)
        for cc in range(ncol):
            act_ref[cc, kt * et:(kt + 1) * et, :] = a_part[:, cc * LANES:(cc + 1) * LANES]

    def per_col(lc, carry):
        for kt in range(PEER_KEY_TILES):
            tile = s * PEER_KEY_TILES + kt
            for i1 in range(SUBLANES):
                r0 = (kt * SUBLANES + i1) * PEER_NKEYS
                rows = slice(r0, r0 + PEER_NKEYS)
                acc = jnp.zeros((PEER_NKEYS, LANES), F32)
                for hh in range(PEER_HEADS):
                    keep = s2_ref[hh, lc] >= th_ref[hh, lc, tile, i1:i1 + 1, :]
                    acc = jnp.where(keep, acc + e1_ref[hh, lc, tile, i1:i1 + 1, :] * e2_ref[hh, lc], acc)
                a = act_ref[lc, rows, :]
                gelu = 0.5 * a * (1.0 + lax.erf(a * math.sqrt(0.5)))
                c_ref[lc, rows, :] = (acc * gelu).astype(BF16)
        return carry

    lax.fori_loop(0, ncol, per_col, 0)
    y_part = None
    for kt in range(PEER_KEY_TILES):
        coef = jnp.concatenate([c_ref[cc, kt * et:(kt + 1) * et, :] for cc in range(ncol)], axis=1)
        p = jnp.dot(vt_ref[:, kt * et:(kt + 1) * et], coef, preferred_element_type=F32)
        y_part = p if y_part is None else y_part + p
    yt_ref[...] += y_part

    @pl.when(s == nch - 1)
    def _():
        hh = h_ref[...] + yt_ref[...].T
        out_ref[...] = _rmsnorm(hh, gfin_ref[...])


def _peer(h1, gffn, wqt, keys, sela, selb, u, vt, gfin, tb):
    n, d = h1.shape
    ne = u.shape[0]
    ec = PEER_KEY_TILES * SUBLANES * PEER_NKEYS
    nch = ne // ec
    assert ne % ec == 0
    nq = wqt.shape[0]
    ncol = tb // LANES
    key_tiles = (PEER_HEADS, ncol, PEER_NKEYS // SUBLANES, SUBLANES, LANES)
    return pl.pallas_call(
        functools.partial(_peer_body, tb=tb, nch=nch),
        grid=(n // tb, nch),
        in_specs=[
            pl.BlockSpec((tb, d), lambda i, s: (i, 0)),
            _full(gffn.shape),
            pl.BlockSpec(wqt.shape, lambda i, s: (0, 0), pipeline_mode=pl.Buffered(1)),
            _full(keys.shape), _full(sela.shape), _full(selb.shape),
            pl.BlockSpec((ec, d), lambda i, s: (s, 0)),
            pl.BlockSpec((d, ec), lambda i, s: (0, s)),
            _full(gfin.shape),
        ],
        out_specs=pl.BlockSpec((tb, d), lambda i, s: (i, 0)),
        out_shape=jax.ShapeDtypeStruct((n, d), F32),
        scratch_shapes=[
            pltpu.VMEM((d, tb), BF16),
            pltpu.VMEM((nq, tb), BF16),
            pltpu.VMEM(key_tiles, F32),
            pltpu.VMEM(key_tiles, F32),
            pltpu.VMEM((PEER_HEADS, ncol, PEER_NKEYS, LANES), F32),
            pltpu.VMEM((PEER_HEADS, ncol, PEER_NKEYS, LANES), F32),
            pltpu.VMEM((2, N_RANK_PAD, LANES), F32),
            pltpu.VMEM((ncol, ec, LANES), F32),
            pltpu.VMEM((ncol, ec, LANES), BF16),
            pltpu.VMEM((d, tb), F32),
        ],
        compiler_params=_params(("parallel", "arbitrary")),
    )(h1, gffn, wqt, keys, sela, selb, u, vt, gfin)


def _block_diag_const(width, block, value):
    idx = jnp.arange(width) // block
    return jnp.where(idx[:, None] == idx[None, :], value, 0.0).astype(BF16)


def _tile(n, pref):
    t = min(n, pref)
    assert n % t == 0, (n, pref)
    return t


def kernel(x, meta_tokens, g_mix, w_in, conv_w, conv_b, conv_ln_g, conv_ln_b, w_conv_out, mu_shift, w0, w_up, a0, a_up, g_up, k_k, k_a, r_k, gn_g, gn_b, w_rwkv_out, w_o, g_ffn, w_q, sub_keys, expert_u, expert_v, g_final):
    depth = g_mix.shape[0]
    assert depth == 1
    b, t, d = x.shape
    n = b * t
    n_meta = meta_tokens.shape[0]
    assert n_meta <= min(HALO, CHUNK) and n_meta % SUBLANES == 0
    w = RWKV_WIDTH
    row = lambda p: p.reshape(1, -1).astype(F32)

    w_in0 = w_in[0]
    wc = w_in0[:, :2 * CONV_CH].astype(BF16)
    wr = w_in0[:, 2 * CONV_CH:2 * CONV_CH + SHIFT_TOTAL].astype(BF16)
    wg = w_in0[:, 2 * CONV_CH + SHIFT_TOTAL:].astype(BF16)
    g_mix0 = row(g_mix[0])
    zeros_w = jnp.zeros((RANK_W, w), F32)
    wup_pad = jnp.concatenate([w_up[0], jnp.zeros((RANK_A, w), F32)], axis=0).astype(BF16)
    aup_pad = jnp.concatenate([zeros_w, a_up[0]], axis=0).astype(BF16)
    ones_bd = _block_diag_const(w, RWKV_HEAD, 1.0)
    avg_bd = _block_diag_const(w, RWKV_HEAD, 1.0 / RWKV_HEAD)
    cw = jnp.concatenate([conv_w[0], jnp.zeros((1, CONV_CH), F32)], axis=0)

    x2d = x.reshape(n, d)
    u, zr, gates = _inproj(x2d, g_mix0, wc, wr, wg, _tile(n, 512))
    u_m, zr_m, _ = _inproj(meta_tokens.astype(F32), g_mix0, wc, wr, wg, n_meta)

    halo0 = jnp.concatenate([jnp.zeros((HALO - n_meta, CONV_CH), F32), u_m], axis=0)
    tt = _tile(t, 256)
    yag = _conv_branch(u.reshape(b, t, CONV_CH), halo0, cw, row(conv_b[0]), row(conv_ln_g[0]), row(conv_ln_b[0]),
                       w_conv_out[0].astype(BF16), gates.reshape(b, t, 2 * d), tt)

    tri = jnp.tril(jnp.ones((CHUNK, CHUNK), F32))
    prep_consts = (row(mu_shift[0]), row(w0[0]), wup_pad, row(a0[0]), aup_pad, g_up[0].astype(BF16),
                   row(k_k[0]), row(k_a[0]), row(r_k[0]), ones_bd)

    def rwkv(z, prev0, s0, tt_):
        tri_bd = jnp.kron(jnp.eye(tt_ // CHUNK, dtype=F32), tri).astype(BF16)
        outs = _rwkv_prep(z, prev0, *prep_consts, tri_bd, tt_)
        at, bt, kt, rt, bh, kh, v, gl, g, bonus = outs
        rh, o0, p, q = _chunk_transforms(at, bt, kt, rt, bh, kh, v)
        o, sfin = _state_scan(rh, o0, p, q, gl, s0)
        return o, g, bonus, sfin

    z_meta = jnp.concatenate([jnp.zeros((CHUNK - n_meta, SHIFT_TOTAL), F32), zr_m], axis=0)[None]
    s_zero = jnp.zeros((w // PAIR_LANES, PAIR_LANES, PAIR_LANES), F32)
    _, _, _, s_meta = rwkv(z_meta, jnp.zeros((1, SHIFT_TOTAL), F32), s_zero, CHUNK)
    o, g, bonus, _ = rwkv(zr.reshape(b, t, SHIFT_TOTAL), zr_m[n_meta - 1:n_meta], s_meta[0], tt)

    h1 = _merge(o.reshape(n, w), bonus.reshape(n, w), g.reshape(n, w), row(gn_g[0]), row(gn_b[0]), avg_bd,
                w_rwkv_out[0].astype(BF16), yag.reshape(n, d), gates, w_o[0].astype(BF16), x2d, _tile(n, 512))

    pairs = _candidate_pairs()
    sela = jnp.zeros((N_CAND_PAD, N_RANK_PAD), F32).at[jnp.arange(N_CAND), jnp.array([p[0] for p in pairs])].set(1.0)
    selb = jnp.zeros((N_CAND_PAD, N_RANK_PAD), F32).at[jnp.arange(N_CAND), jnp.array([p[1] for p in pairs])].set(1.0)
    out = _peer(h1, row(g_ffn[0]), w_q[0].T.astype(BF16), sub_keys[0].astype(BF16),
                sela.astype(BF16), selb.astype(BF16), expert_u[0].astype(BF16), expert_v[0].T.astype(BF16),
                row(g_final), _tile(n, 512))
    return out.reshape(b, t, d)
```

```python
import functools
import math

import jax
import jax.numpy as jnp
from jax import lax
from jax.experimental import pallas as pl
from jax.experimental.pallas import tpu as pltpu

F32 = jnp.float32
BF16 = jnp.bfloat16

CONV_CH = 512
CONV_KERNEL = 31
RWKV_WIDTH = 512
RWKV_HEAD = 64
RWKV_HEADS = RWKV_WIDTH // RWKV_HEAD
RANK_W = 64
RANK_A = 64
RANK_G = 128
SHIFT_TOTAL = 3 * RWKV_WIDTH + RANK_W + RANK_A + RANK_G
PEER_HEADS = 8
PEER_NKEYS = 128
PEER_DHALF = 128
PEER_TOPK = 16
RMS_EPS = 1e-6
LN_EPS = 1e-5
GN_EPS = 64e-5

LANES = 128
SUBLANES = 8
VMEM_LIMIT_BYTES = 58 * 1024 * 1024

CHUNK = 64
HALO = 32
GROUP_LANES = 256
PAIR_LANES = 128

NT_DIMS = (((1,), (1,)), ((), ()))
TN_DIMS = (((0,), (0,)), ((), ()))


def _sigmoid(x):
    return 1.0 / (1.0 + jnp.exp(-x))


def _dot(a, b):
    return jnp.dot(a.astype(BF16), b.astype(BF16), preferred_element_type=F32)


def _dot_nt(a, b):
    return lax.dot_general(a.astype(BF16), b.astype(BF16), NT_DIMS, preferred_element_type=F32)


def _dot_tn(a, b):
    return lax.dot_general(a.astype(BF16), b.astype(BF16), TN_DIMS, preferred_element_type=F32)


def _split3(x):
    hi = x.astype(BF16)
    r1 = x - hi.astype(F32)
    mid = r1.astype(BF16)
    lo = (r1 - mid.astype(F32)).astype(BF16)
    return hi, mid, lo


def _dot_x2(x, w_bf16):
    hi = x.astype(BF16)
    lo = (x - hi.astype(F32)).astype(BF16)
    return jnp.dot(hi, w_bf16, preferred_element_type=F32) + jnp.dot(lo, w_bf16, preferred_element_type=F32)


def _dot_w3(w_bf16, x):
    hi, mid, lo = _split3(x)
    acc = jnp.dot(w_bf16, hi, preferred_element_type=F32)
    acc = acc + jnp.dot(w_bf16, mid, preferred_element_type=F32)
    return acc + jnp.dot(w_bf16, lo, preferred_element_type=F32)


def _rmsnorm(x, g):
    ms = jnp.mean(x * x, axis=-1, keepdims=True)
    return x * lax.rsqrt(ms + RMS_EPS) * g


def _params(semantics):
    return pltpu.CompilerParams(dimension_semantics=semantics, vmem_limit_bytes=VMEM_LIMIT_BYTES)


def _full(shape):
    nd = len(shape)
    return pl.BlockSpec(shape, lambda *_: (0,) * nd)


def _inproj_body(x_ref, g_ref, wc_ref, wr_ref, wg_ref, u_ref, zr_ref, gate_ref):
    xn = _rmsnorm(x_ref[...], g_ref[...]).astype(BF16)
    zc = jnp.dot(xn, wc_ref[...], preferred_element_type=F32)
    u_ref[...] = zc[:, :CONV_CH] * _sigmoid(zc[:, CONV_CH:])
    zr_ref[...] = jnp.dot(xn, wr_ref[...], preferred_element_type=F32)
    gate_ref[...] = _sigmoid(jnp.dot(xn, wg_ref[...], preferred_element_type=F32))


def _inproj(x2d, g, wc, wr, wg, tm):
    n, d = x2d.shape
    ng = wg.shape[1]
    return pl.pallas_call(
        _inproj_body,
        grid=(n // tm,),
        in_specs=[
            pl.BlockSpec((tm, d), lambda i: (i, 0)),
            _full(g.shape), _full(wc.shape), _full(wr.shape), _full(wg.shape),
        ],
        out_specs=[
            pl.BlockSpec((tm, CONV_CH), lambda i: (i, 0)),
            pl.BlockSpec((tm, SHIFT_TOTAL), lambda i: (i, 0)),
            pl.BlockSpec((tm, ng), lambda i: (i, 0)),
        ],
        out_shape=[
            jax.ShapeDtypeStruct((n, CONV_CH), F32),
            jax.ShapeDtypeStruct((n, SHIFT_TOTAL), F32),
            jax.ShapeDtypeStruct((n, ng), F32),
        ],
        compiler_params=_params(("parallel",)),
    )(x2d, g, wc, wr, wg)


CONV_ROWS = 32


def _conv_body(u_ref, halo0_ref, cw_ref, cb_ref, lng_ref, lnb_ref, wo_ref, gate_ref, out_ref,
               ext_ref, sh_ref, y_ref, *, tt):
    t = pl.program_id(1)

    @pl.when(t == 0)
    def _():
        ext_ref[0:HALO, :] = halo0_ref[...]

    @pl.when(t > 0)
    def _():
        ext_ref[0:HALO, :] = ext_ref[tt:tt + HALO, :]

    ext_ref[HALO:HALO + tt, :] = u_ref[0]
    first = HALO - (CONV_KERNEL - 1)
    last_a = [(CONV_KERNEL - 1 - b) // SUBLANES for b in range(SUBLANES)]
    for b in range(SUBLANES):
        nrow = tt + SUBLANES * last_a[b]
        sh_ref[b, 0:nrow, :] = ext_ref[first + b:first + b + nrow, :]
    for r0 in range(0, tt, CONV_ROWS):
        acc = jnp.broadcast_to(cb_ref[...], (CONV_ROWS, CONV_CH))
        for b in range(SUBLANES):
            for a in range(last_a[b] + 1):
                j = SUBLANES * a + b
                acc = acc + sh_ref[b, r0 + SUBLANES * a:r0 + SUBLANES * a + CONV_ROWS, :] * cw_ref[j:j + 1, :]
        y_ref[r0:r0 + CONV_ROWS, :] = acc
    y = y_ref[...]
    mean = jnp.mean(y, axis=-1, keepdims=True)
    d = y - mean
    var = jnp.mean(d * d, axis=-1, keepdims=True)
    yn = d * lax.rsqrt(var + LN_EPS) * lng_ref[...] + lnb_ref[...]
    act = yn * _sigmoid(yn)
    ya = jnp.dot(act.astype(BF16), wo_ref[...], preferred_element_type=F32)
    out_ref[0] = ya * gate_ref[0]


def _conv_branch(u, halo0, cw, cb, lng, lnb, wo, gates, tt):
    b, t, _ = u.shape
    d = wo.shape[1]
    return pl.pallas_call(
        functools.partial(_conv_body, tt=tt),
        grid=(b, t // tt),
        in_specs=[
            pl.BlockSpec((1, tt, CONV_CH), lambda i, j: (i, j, 0)),
            _full(halo0.shape), _full(cw.shape), _full(cb.shape), _full(lng.shape), _full(lnb.shape),
            _full(wo.shape),
            pl.BlockSpec((1, tt, d), lambda i, j: (i, j, 0)),
        ],
        out_specs=pl.BlockSpec((1, tt, d), lambda i, j: (i, j, 0)),
        out_shape=jax.ShapeDtypeStruct((b, t, d), F32),
        scratch_shapes=[pltpu.VMEM((tt + HALO, CONV_CH), F32),
                        pltpu.VMEM((SUBLANES, tt + HALO, CONV_CH), F32),
                        pltpu.VMEM((tt, CONV_CH), F32)],
        compiler_params=_params(("parallel", "arbitrary")),
    )(u, halo0, cw, cb, lng, lnb, wo, gates)


def _rwkv_prep_body(z_ref, prev0_ref, mu_ref, w0_ref, wup_ref, a0_ref, aup_ref, gup_ref,
                    kk_ref, ka_ref, rk_ref, ones_ref, tri_ref,
                    at_ref, bt_ref, kt_ref, rt_ref, bh_ref, kh_ref, v_ref, gl_ref, g_ref, bonus_ref,
                    ext_ref, *, tt):
    t = pl.program_id(1)
    w = RWKV_WIDTH

    @pl.when(t == 0)
    def _():
        ext_ref[SUBLANES - 1:SUBLANES, :] = prev0_ref[...]

    @pl.when(t > 0)
    def _():
        ext_ref[SUBLANES - 1:SUBLANES, :] = ext_ref[SUBLANES + tt - 1:SUBLANES + tt, :]

    z = z_ref[0]
    ext_ref[SUBLANES:SUBLANES + tt, :] = z
    zprev = ext_ref[SUBLANES - 1:SUBLANES - 1 + tt, :]
    zs = z + (zprev - z) * mu_ref[...]
    r = zs[:, 0:w]
    k = zs[:, w:2 * w]
    v = zs[:, 2 * w:3 * w]
    lwla = zs[:, 3 * w:3 * w + RANK_W + RANK_A]
    lg = zs[:, 3 * w + RANK_W + RANK_A:]
    wlin = w0_ref[...] + jnp.dot(jnp.tanh(lwla).astype(BF16), wup_ref[...], preferred_element_type=F32)
    x = -wlin
    softplus = jnp.maximum(x, 0.0) + jnp.log(1.0 + jnp.exp(-jnp.abs(x)))
    wlog = -softplus - 0.5
    logdecay = -jnp.exp(wlog)
    a = _sigmoid(a0_ref[...] + jnp.dot(lwla.astype(BF16), aup_ref[...], preferred_element_type=F32))
    g = jnp.dot(_sigmoid(lg).astype(BF16), gup_ref[...], preferred_element_type=F32)
    kk = k * kk_ref[...]
    ss = _dot_x2(kk * kk, ones_ref[...])
    kk = kk * lax.rsqrt(jnp.maximum(ss, 1e-24))
    k2 = k * (1.0 + (a - 1.0) * ka_ref[...])
    bonus_ref[0] = _dot_x2(r * k2 * rk_ref[...], ones_ref[...]) * v
    g_ref[0] = g

    cs = _dot_w3(tri_ref[...], logdecay)
    nchunk = tt // CHUNK
    cs3 = cs.reshape(nchunk, CHUNK, w)
    csl3 = cs3[:, CHUNK - 1:CHUNK, :]
    csl = jnp.broadcast_to(csl3, (nchunk, CHUNK, w)).reshape(tt, w)
    gl_ref[0] = jnp.broadcast_to(jnp.exp(csl3), (nchunk, SUBLANES, w))
    inv_g = jnp.exp(-cs)
    to_end = jnp.exp(csl - cs)
    avec = -kk
    bvec = kk * a
    at_ref[0] = avec * jnp.exp(cs - logdecay)
    bt_ref[0] = bvec * inv_g
    kt_ref[0] = k2 * inv_g
    rt_ref[0] = r * jnp.exp(cs)
    bh_ref[0] = bvec * to_end
    kh_ref[0] = k2 * to_end
    v_ref[0] = v


def _rwkv_prep(z, prev0, mu, w0, wup, a0, aup, gup, k_k, k_a, r_k, ones_bd, tri_bd, tt):
    b, t, _ = z.shape
    w = RWKV_WIDTH
    seq = pl.BlockSpec((1, tt, w), lambda i, j: (i, j, 0))
    seq_shape = jax.ShapeDtypeStruct((b, t, w), F32)
    consts = (prev0, mu, w0, wup, a0, aup, gup, k_k, k_a, r_k, ones_bd, tri_bd)
    return pl.pallas_call(
        functools.partial(_rwkv_prep_body, tt=tt),
        grid=(b, t // tt),
        in_specs=[pl.BlockSpec((1, tt, SHIFT_TOTAL), lambda i, j: (i, j, 0))] + [_full(c.shape) for c in consts],
        out_specs=[seq] * 7 + [pl.BlockSpec((1, tt // CHUNK, SUBLANES, w), lambda i, j: (i, j, 0, 0)), seq, seq],
        out_shape=[seq_shape] * 7 + [jax.ShapeDtypeStruct((b, t // CHUNK, SUBLANES, w), F32), seq_shape, seq_shape],
        scratch_shapes=[pltpu.VMEM((tt + SUBLANES, SHIFT_TOTAL), F32)],
        compiler_params=_params(("parallel", "arbitrary")),
    )(z, *consts)


def _chunk_body(at_ref, bt_ref, kt_ref, rt_ref, bh_ref, kh_ref, v_ref,
                rh_ref, o0_ref, p_ref, q_ref, *, cps):
    gw = GROUP_LANES
    heads = gw // RWKV_HEAD
    rows = heads * CHUNK
    ri = lax.broadcasted_iota(jnp.int32, (rows, rows), 0)
    ci = lax.broadcasted_iota(jnp.int32, (rows, rows), 1)
    same = (ri // CHUNK) == (ci // CHUNK)
    strict = same & (ci < ri)
    incl = same & (ci <= ri)
    eye = ri == ci
    lane = lax.broadcasted_iota(jnp.int32, (1, gw), 1) // RWKV_HEAD
    lri = lax.broadcasted_iota(jnp.int32, (gw, gw), 0)
    lci = lax.broadcasted_iota(jnp.int32, (gw, gw), 1)
    lsame = (lri // RWKV_HEAD) == (lci // RWKV_HEAD)

    def stack(x):
        return jnp.concatenate([jnp.where(lane == i, x, 0.0) for i in range(heads)], axis=0)

    def unstack(x):
        out = x[0:CHUNK]
        for i in range(1, heads):
            out = out + x[i * CHUNK:(i + 1) * CHUNK]
        return out

    chains = [(ck, grp) for ck in range(cps) for grp in range(RWKV_WIDTH // gw)]

    def load(ref):
        return [ref[0, ck * CHUNK:(ck + 1) * CHUNK, grp * gw:(grp + 1) * gw] for ck, grp in chains]

    def each(fn, *lists):
        return [fn(*args) for args in zip(*lists)]

    at, bt, kt, rt, bh, kh, v = (load(r) for r in (at_ref, bt_ref, kt_ref, rt_ref, bh_ref, kh_ref, v_ref))
    xa, xr, yb, yk, vs = (each(stack, x) for x in (at, rt, bt, kt, v))
    a_ab = each(lambda x, y: jnp.where(strict, _dot_nt(x, y), 0.0), xa, yb)
    a_ak = each(lambda x, y: jnp.where(strict, _dot_nt(x, y), 0.0), xa, yk)
    a_rb = each(lambda x, y: jnp.where(incl, _dot_nt(x, y), 0.0), xr, yb)
    a_rk = each(lambda x, y: jnp.where(incl, _dot_nt(x, y), 0.0), xr, yk)
    tm = each(lambda a: jnp.where(eye, 1.0, 0.0) + a, a_ab)
    apow = a_ab
    for _ in range(int(math.log2(CHUNK)) - 1):
        apow = each(lambda a: _dot(a, a), apow)
        tm = each(lambda t, a: t + _dot(t, a), tm, apow)
    akv = each(_dot, a_ak, vs)
    tw = each(lambda t, x, y: _dot(t, jnp.concatenate([x, y], axis=1)), tm, xa, akv)
    rb = each(_dot, a_rb, tw)
    ov = each(_dot, a_rk, vs)
    rhat = each(lambda x, r: unstack(x + r[:, :gw]), xr, rb)
    o0 = each(lambda o, r: unstack(o + r[:, gw:]), ov, rb)
    ahat = each(lambda t: unstack(t[:, :gw]), tw)
    u0 = each(lambda t: unstack(t[:, gw:]), tw)
    pm = each(lambda a, b: jnp.where(lsame, _dot_tn(a, b), 0.0), ahat, bh)
    qm = each(lambda u, b, vv, k: jnp.where(lsame, _dot_tn(u, b) + _dot_tn(vv, k), 0.0), u0, bh, v, kh)
    for i, (ck, grp) in enumerate(chains):
        sl = slice(grp * gw, (grp + 1) * gw)
        tr = slice(ck * CHUNK, (ck + 1) * CHUNK)
        rh_ref[0, tr, sl] = rhat[i]
        o0_ref[0, tr, sl] = o0[i]
        for pr in range(gw // PAIR_LANES):
            ps = slice(pr * PAIR_LANES, (pr + 1) * PAIR_LANES)
            p_ref[0, ck, grp * (gw // PAIR_LANES) + pr] = pm[i][ps, ps]
            q_ref[0, ck, grp * (gw // PAIR_LANES) + pr] = qm[i][ps, ps]


def _chunk_transforms(at, bt, kt, rt, bh, kh, v):
    b, t, w = at.shape
    nc = t // CHUNK
    cps = 4 if nc % 4 == 0 else 1
    npair = w // PAIR_LANES
    seq = pl.BlockSpec((1, cps * CHUNK, w), lambda i, j: (i, j, 0))
    mat = pl.BlockSpec((1, cps, npair, PAIR_LANES, PAIR_LANES), lambda i, j: (i, j, 0, 0, 0))
    return pl.pallas_call(
        functools.partial(_chunk_body, cps=cps),
        grid=(b, nc // cps),
        in_specs=[seq] * 7,
        out_specs=[seq, seq, mat, mat],
        out_shape=[
            jax.ShapeDtypeStruct((b, t, w), F32),
            jax.ShapeDtypeStruct((b, t, w), F32),
            jax.ShapeDtypeStruct((b, nc, npair, PAIR_LANES, PAIR_LANES), F32),
            jax.ShapeDtypeStruct((b, nc, npair, PAIR_LANES, PAIR_LANES), F32),
        ],
        compiler_params=_params(("parallel", "parallel")),
    )(at, bt, kt, rt, bh, kh, v)


def _scan_body(rh_ref, o0_ref, p_ref, q_ref, gl_ref, s0_ref, o_ref, sfin_ref, s_ref):
    c = pl.program_id(0)
    nb, npair = s_ref.shape[0], s_ref.shape[1]

    @pl.when(c == 0)
    def _():
        for bi in range(nb):
            s_ref[bi] = s0_ref[...]

    for bi in range(nb):
        for pr in range(npair):
            ps = slice(pr * PAIR_LANES, (pr + 1) * PAIR_LANES)
            s = s_ref[bi, pr]
            s_hi = s.astype(BF16)
            s_lo = (s - s_hi.astype(F32)).astype(BF16)
            o_ref[bi, :, ps] = _dot_nt(rh_ref[bi, :, ps], s_hi) + o0_ref[bi, :, ps]
            p = p_ref[bi, 0, pr].astype(BF16)
            s_ref[bi, pr] = (s * gl_ref[bi, 0, 0:1, ps]
                             + jnp.dot(s_hi, p, preferred_element_type=F32)
                             + jnp.dot(s_lo, p, preferred_element_type=F32)
                             + q_ref[bi, 0, pr])

    @pl.when(c == pl.num_programs(0) - 1)
    def _():
        sfin_ref[...] = s_ref[...]


def _state_scan(rh, o0, p, q, gl, s0):
    b, t, w = rh.shape
    nc = t // CHUNK
    npair = w // PAIR_LANES
    seq = pl.BlockSpec((b, CHUNK, w), lambda c: (0, c, 0))
    mat = pl.BlockSpec((b, 1, npair, PAIR_LANES, PAIR_LANES), lambda c: (0, c, 0, 0, 0))
    return pl.pallas_call(
        _scan_body,
        grid=(nc,),
        in_specs=[seq, seq, mat, mat, pl.BlockSpec((b, 1, SUBLANES, w), lambda c: (0, c, 0, 0)), _full(s0.shape)],
        out_specs=[seq, _full((b, npair, PAIR_LANES, PAIR_LANES))],
        out_shape=[
            jax.ShapeDtypeStruct((b, t, w), F32),
            jax.ShapeDtypeStruct((b, npair, PAIR_LANES, PAIR_LANES), F32),
        ],
        scratch_shapes=[pltpu.VMEM((b, npair, PAIR_LANES, PAIR_LANES), F32)],
        compiler_params=_params(("arbitrary",)),
    )(rh, o0, p, q, gl, s0)


def _merge_body(o_ref, bonus_ref, g_ref, gng_ref, gnb_ref, avg_ref, wro_ref, yag_ref, gateb_ref, wo_ref, x_ref,
                h_ref):
    o = o_ref[...]
    mean = _dot_x2(o, avg_ref[...])
    d = o - mean
    var = _dot_x2(d * d, avg_ref[...])
    on = d * lax.rsqrt(var + GN_EPS) * gng_ref[...] + gnb_ref[...]
    xo = (on + bonus_ref[...]) * g_ref[...]
    yb = jnp.dot(xo.astype(BF16), wro_ref[...], preferred_element_type=F32)
    m = yag_ref[...] + gateb_ref[...] * yb
    h_ref[...] = x_ref[...] + jnp.dot(m.astype(BF16), wo_ref[...], preferred_element_type=F32)


def _merge(o, bonus, g, gng, gnb, avg_bd, wro, yag, gates, wo, x2d, tm):
    n, d = x2d.shape
    w = RWKV_WIDTH
    row_w = pl.BlockSpec((tm, w), lambda i: (i, 0))
    row_d = pl.BlockSpec((tm, d), lambda i: (i, 0))
    return pl.pallas_call(
        _merge_body,
        grid=(n // tm,),
        in_specs=[row_w, row_w, row_w, _full(gng.shape), _full(gnb.shape), _full(avg_bd.shape), _full(wro.shape),
                  row_d, pl.BlockSpec((tm, d), lambda i: (i, 1)), _full(wo.shape), row_d],
        out_specs=row_d,
        out_shape=jax.ShapeDtypeStruct((n, d), F32),
        compiler_params=_params(("parallel",)),
    )(o, bonus, g, gng, gnb, avg_bd, wro, yag, gates, wo, x2d)


PEER_KEY_TILES = 2
N_RANK = PEER_TOPK + 1
N_RANK_PAD = -(-N_RANK // (2 * SUBLANES)) * (2 * SUBLANES)


def _candidate_pairs():
    return [(i, j) for i in range(N_RANK) for j in range(N_RANK) if (i + 1) * (j + 1) <= N_RANK]


N_CAND = len(_candidate_pairs())
N_CAND_PAD = -(-N_CAND // (2 * SUBLANES)) * (2 * SUBLANES)


def _peer_body(h_ref, gffn_ref, wqt_ref, keys_ref, sela_ref, selb_ref, u_ref, vt_ref, gfin_ref, out_ref,
               xnt_ref, qt_ref, th_ref, e1_ref, s2_ref, e2_ref, top_ref, act_ref, c_ref, yt_ref,
               *, tb, nch):
    s = pl.program_id(1)
    ncol = tb // LANES
    neg_inf = -1e30

    def top_values(x):
        vals = []
        cur = x
        for _ in range(N_RANK):
            m = jnp.max(cur, axis=0, keepdims=True)
            vals.append(m)
            cur = jnp.where(cur == m, neg_inf, cur)
        return vals

    def prep():
        xnt = _rmsnorm(h_ref[...], gffn_ref[...]).T.astype(BF16)
        xnt_ref[...] = xnt
        qt_ref[...] = jnp.dot(wqt_ref[...], xnt, preferred_element_type=F32).astype(BF16)
        yt_ref[...] = jnp.zeros_like(yt_ref)
        top_ref[...] = jnp.zeros_like(top_ref)

        def per_head(hh, carry):
            base = pl.multiple_of(hh * (2 * PEER_DHALF), 2 * PEER_DHALF)
            s1 = jnp.dot(keys_ref[hh, 0], qt_ref[pl.ds(base, PEER_DHALF), :], preferred_element_type=F32)
            s2 = jnp.dot(keys_ref[hh, 1], qt_ref[pl.ds(base + PEER_DHALF, PEER_DHALF), :],
                         preferred_element_type=F32)
            for lc in range(ncol):
                ls = slice(lc * LANES, (lc + 1) * LANES)
                s1c = s1[:, ls]
                s2c = s2[:, ls]
                va = top_values(s1c)
                vb = top_values(s2c)
                for i in range(N_RANK):
                    top_ref[0, i:i + 1, :] = va[i]
                    top_ref[1, i:i + 1, :] = vb[i]
                row = lax.broadcasted_iota(jnp.int32, (N_CAND_PAD, LANES), 0)
                asel = _dot_w3(sela_ref[...], top_ref[0])
                bsel = _dot_w3(selb_ref[...], top_ref[1])
                cand = jnp.where(row < N_CAND, asel + bsel, neg_inf)
                best = top_values(cand)
                tau = 0.5 * (best[PEER_TOPK - 1] + best[PEER_TOPK])
                keepc = (bsel >= tau - asel) & (row < N_CAND)
                zsum = jnp.sum(jnp.where(keepc, jnp.exp(asel - va[0]) * jnp.exp(bsel - vb[0]), 0.0),
                               axis=0, keepdims=True)
                grouped = (PEER_NKEYS // SUBLANES, SUBLANES, LANES)
                th_ref[hh, lc] = (tau - s1c).reshape(grouped)
                e1_ref[hh, lc] = jnp.exp(s1c - va[0]).reshape(grouped)
                s2_ref[hh, lc] = s2c
                e2_ref[hh, lc] = jnp.exp(s2c - vb[0]) * (0.5 / zsum)
            return carry

        lax.fori_loop(0, PEER_HEADS, per_head, 0)

    @pl.when(s == 0)
    def _():
        prep()

    et = SUBLANES * PEER_NKEYS
    for kt in range(PEER_KEY_TILES):
        a_part = jnp.dot(u_ref[kt * et:(kt + 1) * et, :], xnt_ref[...], preferred_element_type=F32)
        for cc in range(ncol):
            act_ref[cc, kt * et:(kt + 1) * et, :] = a_part[:, cc * LANES:(cc + 1) * LANES]

    def per_col(lc, carry):
        for kt in range(PEER_KEY_TILES):
            tile = s * PEER_KEY_TILES + kt
            for i1 in range(SUBLANES):
                r0 = (kt * SUBLANES + i1) * PEER_NKEYS
                rows = slice(r0, r0 + PEER_NKEYS)
                acc = jnp.zeros((PEER_NKEYS, LANES), F32)
                for hh in range(PEER_HEADS):
                    keep = s2_ref[hh, lc] >= th_ref[hh, lc, tile, i1:i1 + 1, :]
                    acc = jnp.where(keep, acc + e1_ref[hh, lc, tile, i1:i1 + 1, :] * e2_ref[hh, lc], acc)
                a = act_ref[lc, rows, :]
                gelu2 = a + a * lax.erf(a * math.sqrt(0.5))
                c_ref[lc, rows, :] = (acc * gelu2).astype(BF16)
        return carry

    lax.fori_loop(0, ncol, per_col, 0)
    y_part = None
    for kt in range(PEER_KEY_TILES):
        coef = jnp.concatenate([c_ref[cc, kt * et:(kt + 1) * et, :] for cc in range(ncol)], axis=1)
        p = jnp.dot(vt_ref[:, kt * et:(kt + 1) * et], coef, preferred_element_type=F32)
        y_part = p if y_part is None else y_part + p
    yt_ref[...] += y_part

    @pl.when(s == nch - 1)
    def _():
        hh = h_ref[...] + yt_ref[...].T
        out_ref[...] = _rmsnorm(hh, gfin_ref[...])


def _peer(h1, gffn, wqt, keys, sela, selb, u, vt, gfin, tb):
    n, d = h1.shape
    ne = u.shape[0]
    ec = PEER_KEY_TILES * SUBLANES * PEER_NKEYS
    nch = ne // ec
    assert ne % ec == 0
    nq = wqt.shape[0]
    ncol = tb // LANES
    key_tiles = (PEER_HEADS, ncol, PEER_NKEYS // SUBLANES, SUBLANES, LANES)
    return pl.pallas_call(
        functools.partial(_peer_body, tb=tb, nch=nch),
        grid=(n // tb, nch),
        in_specs=[
            pl.BlockSpec((tb, d), lambda i, s: (i, 0)),
            _full(gffn.shape),
            pl.BlockSpec(wqt.shape, lambda i, s: (0, 0), pipeline_mode=pl.Buffered(1)),
            _full(keys.shape), _full(sela.shape), _full(selb.shape),
            pl.BlockSpec((ec, d), lambda i, s: (s, 0)),
            pl.BlockSpec((d, ec), lambda i, s: (0, s)),
            _full(gfin.shape),
        ],
        out_specs=pl.BlockSpec((tb, d), lambda i, s: (i, 0)),
        out_shape=jax.ShapeDtypeStruct((n, d), F32),
        scratch_shapes=[
            pltpu.VMEM((d, tb), BF16),
            pltpu.VMEM((nq, tb), BF16),
            pltpu.VMEM(key_tiles, F32),
            pltpu.VMEM(key_tiles, F32),
            pltpu.VMEM((PEER_HEADS, ncol, PEER_NKEYS, LANES), F32),
            pltpu.VMEM((PEER_HEADS, ncol, PEER_NKEYS, LANES), F32),
            pltpu.VMEM((2, N_RANK_PAD, LANES), F32),
            pltpu.VMEM((ncol, ec, LANES), F32),
            pltpu.VMEM((ncol, ec, LANES), BF16),
            pltpu.VMEM((d, tb), F32),
        ],
        compiler_params=_params(("parallel", "arbitrary")),
    )(h1, gffn, wqt, keys, sela, selb, u, vt, gfin)


def _block_diag_const(width, block, value):
    idx = jnp.arange(width) // block
    return jnp.where(idx[:, None] == idx[None, :], value, 0.0).astype(BF16)


def _tile(n, pref):
    t = min(n, pref)
    assert n % t == 0, (n, pref)
    return t


def kernel(x, meta_tokens, g_mix, w_in, conv_w, conv_b, conv_ln_g, conv_ln_b, w_conv_out, mu_shift, w0, w_up, a0, a_up, g_up, k_k, k_a, r_k, gn_g, gn_b, w_rwkv_out, w_o, g_ffn, w_q, sub_keys, expert_u, expert_v, g_final):
    depth = g_mix.shape[0]
    assert depth == 1
    b, t, d = x.shape
    n = b * t
    n_meta = meta_tokens.shape[0]
    assert n_meta <= min(HALO, CHUNK) and n_meta % SUBLANES == 0
    w = RWKV_WIDTH
    row = lambda p: p.reshape(1, -1).astype(F32)

    w_in0 = w_in[0]
    wc = w_in0[:, :2 * CONV_CH].astype(BF16)
    wr = w_in0[:, 2 * CONV_CH:2 * CONV_CH + SHIFT_TOTAL].astype(BF16)
    wg = w_in0[:, 2 * CONV_CH + SHIFT_TOTAL:].astype(BF16)
    g_mix0 = row(g_mix[0])
    zeros_w = jnp.zeros((RANK_W, w), F32)
    wup_pad = jnp.concatenate([w_up[0], jnp.zeros((RANK_A, w), F32)], axis=0).astype(BF16)
    aup_pad = jnp.concatenate([zeros_w, a_up[0]], axis=0).astype(BF16)
    ones_bd = _block_diag_const(w, RWKV_HEAD, 1.0)
    avg_bd = _block_diag_const(w, RWKV_HEAD, 1.0 / RWKV_HEAD)
    cw = jnp.concatenate([conv_w[0], jnp.zeros((1, CONV_CH), F32)], axis=0)

    x2d = x.reshape(n, d)
    u, zr, gates = _inproj(x2d, g_mix0, wc, wr, wg, _tile(n, 512))
    u_m, zr_m, _ = _inproj(meta_tokens.astype(F32), g_mix0, wc, wr, wg, n_meta)

    halo0 = jnp.concatenate([jnp.zeros((HALO - n_meta, CONV_CH), F32), u_m], axis=0)
    tt = _tile(t, 256)
    yag = _conv_branch(u.reshape(b, t, CONV_CH), halo0, cw, row(conv_b[0]), row(conv_ln_g[0]), row(conv_ln_b[0]),
                       w_conv_out[0].astype(BF16), gates.reshape(b, t, 2 * d), tt)

    tri = jnp.tril(jnp.ones((CHUNK, CHUNK), F32))
    prep_consts = (row(mu_shift[0]), row(w0[0]), wup_pad, row(a0[0]), aup_pad, g_up[0].astype(BF16),
                   row(k_k[0]), row(k_a[0]), row(r_k[0]), ones_bd)

    def rwkv(z, prev0, s0, tt_):
        tri_bd = jnp.kron(jnp.eye(tt_ // CHUNK, dtype=F32), tri).astype(BF16)
        outs = _rwkv_prep(z, prev0, *prep_consts, tri_bd, tt_)
        at, bt, kt, rt, bh, kh, v, gl, g, bonus = outs
        rh, o0, p, q = _chunk_transforms(at, bt, kt, rt, bh, kh, v)
        o, sfin = _state_scan(rh, o0, p, q, gl, s0)
        return o, g, bonus, sfin

    z_meta = jnp.concatenate([jnp.zeros((CHUNK - n_meta, SHIFT_TOTAL), F32), zr_m], axis=0)[None]
    s_zero = jnp.zeros((w // PAIR_LANES, PAIR_LANES, PAIR_LANES), F32)
    _, _, _, s_meta = rwkv(z_meta, jnp.zeros((1, SHIFT_TOTAL), F32), s_zero, CHUNK)
    o, g, bonus, _ = rwkv(zr.reshape(b, t, SHIFT_TOTAL), zr_m[n_meta - 1:n_meta], s_meta[0], tt)

    h1 = _merge(o.reshape(n, w), bonus.reshape(n, w), g.reshape(n, w), row(gn_g[0]), row(gn_b[0]), avg_bd,
                w_rwkv_out[0].astype(BF16), yag.reshape(n, d), gates, w_o[0].astype(BF16), x2d, _tile(n, 512))

    pairs = _candidate_pairs()
    sela = jnp.zeros((N_CAND_PAD, N_RANK_PAD), F32).at[jnp.arange(N_CAND), jnp.array([p[0] for p in pairs])].set(1.0)
    selb = jnp.zeros((N_CAND_PAD, N_RANK_PAD), F32).at[jnp.arange(N_CAND), jnp.array([p[1] for p in pairs])].set(1.0)
    out = _peer(h1, row(g_ffn[0]), w_q[0].T.astype(BF16), sub_keys[0].astype(BF16),
                sela.astype(BF16), selb.astype(BF16), expert_u[0].astype(BF16), expert_v[0].T.astype(BF16),
                row(g_final), _tile(n, 512))
    return out.reshape(b, t, d)
```

```python
import functools
import math

import jax
import jax.numpy as jnp
from jax import lax
from jax.experimental import pallas as pl
from jax.experimental.pallas import tpu as pltpu

F32 = jnp.float32
BF16 = jnp.bfloat16

CONV_CH = 512
CONV_KERNEL = 31
RWKV_WIDTH = 512
RWKV_HEAD = 64
RWKV_HEADS = RWKV_WIDTH // RWKV_HEAD
RANK_W = 64
RANK_A = 64
RANK_G = 128
SHIFT_TOTAL = 3 * RWKV_WIDTH + RANK_W + RANK_A + RANK_G
PEER_HEADS = 8
PEER_NKEYS = 128
PEER_DHALF = 128
PEER_TOPK = 16
RMS_EPS = 1e-6
LN_EPS = 1e-5
GN_EPS = 64e-5

LANES = 128
SUBLANES = 8
VMEM_LIMIT_BYTES = 58 * 1024 * 1024

CHUNK = 64
HALO = 32
GROUP_LANES = 256
PAIR_LANES = 128

NT_DIMS = (((1,), (1,)), ((), ()))
TN_DIMS = (((0,), (0,)), ((), ()))


def _sigmoid(x):
    return 1.0 / (1.0 + jnp.exp(-x))


def _dot(a, b):
    return jnp.dot(a.astype(BF16), b.astype(BF16), preferred_element_type=F32)


def _dot_nt(a, b):
    return lax.dot_general(a.astype(BF16), b.astype(BF16), NT_DIMS, preferred_element_type=F32)


def _dot_tn(a, b):
    return lax.dot_general(a.astype(BF16), b.astype(BF16), TN_DIMS, preferred_element_type=F32)


def _split3(x):
    hi = x.astype(BF16)
    r1 = x - hi.astype(F32)
    mid = r1.astype(BF16)
    lo = (r1 - mid.astype(F32)).astype(BF16)
    return hi, mid, lo


def _dot_x2(x, w_bf16):
    hi = x.astype(BF16)
    lo = (x - hi.astype(F32)).astype(BF16)
    return jnp.dot(hi, w_bf16, preferred_element_type=F32) + jnp.dot(lo, w_bf16, preferred_element_type=F32)


def _dot_w3(w_bf16, x):
    hi, mid, lo = _split3(x)
    acc = jnp.dot(w_bf16, hi, preferred_element_type=F32)
    acc = acc + jnp.dot(w_bf16, mid, preferred_element_type=F32)
    return acc + jnp.dot(w_bf16, lo, preferred_element_type=F32)


def _rmsnorm(x, g):
    ms = jnp.mean(x * x, axis=-1, keepdims=True)
    return x * lax.rsqrt(ms + RMS_EPS) * g


def _params(semantics):
    return pltpu.CompilerParams(dimension_semantics=semantics, vmem_limit_bytes=VMEM_LIMIT_BYTES)


def _full(shape):
    nd = len(shape)
    return pl.BlockSpec(shape, lambda *_: (0,) * nd)


def _inproj_body(x_ref, g_ref, wc_ref, wr_ref, wg_ref, u_ref, zr_ref, gate_ref):
    xn = _rmsnorm(x_ref[...], g_ref[...]).astype(BF16)
    zc = jnp.dot(xn, wc_ref[...], preferred_element_type=F32)
    u_ref[...] = zc[:, :CONV_CH] * _sigmoid(zc[:, CONV_CH:])
    zr_ref[...] = jnp.dot(xn, wr_ref[...], preferred_element_type=F32)
    gate_ref[...] = _sigmoid(jnp.dot(xn, wg_ref[...], preferred_element_type=F32))


def _inproj(x2d, g, wc, wr, wg, tm):
    n, d = x2d.shape
    ng = wg.shape[1]
    return pl.pallas_call(
        _inproj_body,
        grid=(n // tm,),
        in_specs=[
            pl.BlockSpec((tm, d), lambda i: (i, 0)),
            _full(g.shape), _full(wc.shape), _full(wr.shape), _full(wg.shape),
        ],
        out_specs=[
            pl.BlockSpec((tm, CONV_CH), lambda i: (i, 0)),
            pl.BlockSpec((tm, SHIFT_TOTAL), lambda i: (i, 0)),
            pl.BlockSpec((tm, ng), lambda i: (i, 0)),
        ],
        out_shape=[
            jax.ShapeDtypeStruct((n, CONV_CH), F32),
            jax.ShapeDtypeStruct((n, SHIFT_TOTAL), F32),
            jax.ShapeDtypeStruct((n, ng), F32),
        ],
        compiler_params=_params(("parallel",)),
    )(x2d, g, wc, wr, wg)


CONV_ROWS = 32


def _conv_body(u_ref, halo0_ref, cw_ref, cb_ref, lng_ref, lnb_ref, wo_ref, gate_ref, out_ref,
               ext_ref, sh_ref, y_ref, *, tt):
    t = pl.program_id(1)

    @pl.when(t == 0)
    def _():
        ext_ref[0:HALO, :] = halo0_ref[...]

    @pl.when(t > 0)
    def _():
        ext_ref[0:HALO, :] = ext_ref[tt:tt + HALO, :]

    ext_ref[HALO:HALO + tt, :] = u_ref[0]
    first = HALO - (CONV_KERNEL - 1)
    last_a = [(CONV_KERNEL - 1 - b) // SUBLANES for b in range(SUBLANES)]
    for b in range(SUBLANES):
        nrow = tt + SUBLANES * last_a[b]
        sh_ref[b, 0:nrow, :] = ext_ref[first + b:first + b + nrow, :]
    for r0 in range(0, tt, CONV_ROWS):
        acc = jnp.broadcast_to(cb_ref[...], (CONV_ROWS, CONV_CH))
        for b in range(SUBLANES):
            for a in range(last_a[b] + 1):
                j = SUBLANES * a + b
                acc = acc + sh_ref[b, r0 + SUBLANES * a:r0 + SUBLANES * a + CONV_ROWS, :] * cw_ref[j:j + 1, :]
        y_ref[r0:r0 + CONV_ROWS, :] = acc
    y = y_ref[...]
    mean = jnp.mean(y, axis=-1, keepdims=True)
    d = y - mean
    var = jnp.mean(d * d, axis=-1, keepdims=True)
    yn = d * lax.rsqrt(var + LN_EPS) * lng_ref[...] + lnb_ref[...]
    act = yn * _sigmoid(yn)
    ya = jnp.dot(act.astype(BF16), wo_ref[...], preferred_element_type=F32)
    out_ref[0] = ya * gate_ref[0]


def _conv_branch(u, halo0, cw, cb, lng, lnb, wo, gates, tt):
    b, t, _ = u.shape
    d = wo.shape[1]
    return pl.pallas_call(
        functools.partial(_conv_body, tt=tt),
        grid=(b, t // tt),
        in_specs=[
            pl.BlockSpec((1, tt, CONV_CH), lambda i, j: (i, j, 0)),
            _full(halo0.shape), _full(cw.shape), _full(cb.shape), _full(lng.shape), _full(lnb.shape),
            _full(wo.shape),
            pl.BlockSpec((1, tt, d), lambda i, j: (i, j, 0)),
        ],
        out_specs=pl.BlockSpec((1, tt, d), lambda i, j: (i, j, 0)),
        out_shape=jax.ShapeDtypeStruct((b, t, d), F32),
        scratch_shapes=[pltpu.VMEM((tt + HALO, CONV_CH), F32),
                        pltpu.VMEM((SUBLANES, tt + HALO, CONV_CH), F32),
                        pltpu.VMEM((tt, CONV_CH), F32)],
        compiler_params=_params(("parallel", "arbitrary")),
    )(u, halo0, cw, cb, lng, lnb, wo, gates)


def _rwkv_prep_body(z_ref, prev0_ref, mu_ref, w0_ref, wup_ref, a0_ref, aup_ref, gup_ref,
                    kk_ref, ka_ref, rk_ref, ones_ref, tri_ref,
                    at_ref, bt_ref, kt_ref, rt_ref, bh_ref, kh_ref, v_ref, gl_ref, g_ref, bonus_ref,
                    ext_ref, *, tt):
    t = pl.program_id(1)
    w = RWKV_WIDTH

    @pl.when(t == 0)
    def _():
        ext_ref[SUBLANES - 1:SUBLANES, :] = prev0_ref[...]

    @pl.when(t > 0)
    def _():
        ext_ref[SUBLANES - 1:SUBLANES, :] = ext_ref[SUBLANES + tt - 1:SUBLANES + tt, :]

    z = z_ref[0]
    ext_ref[SUBLANES:SUBLANES + tt, :] = z
    zprev = ext_ref[SUBLANES - 1:SUBLANES - 1 + tt, :]
    zs = z + (zprev - z) * mu_ref[...]
    r = zs[:, 0:w]
    k = zs[:, w:2 * w]
    v = zs[:, 2 * w:3 * w]
    lwla = zs[:, 3 * w:3 * w + RANK_W + RANK_A]
    lg = zs[:, 3 * w + RANK_W + RANK_A:]
    wlin = w0_ref[...] + jnp.dot(jnp.tanh(lwla).astype(BF16), wup_ref[...], preferred_element_type=F32)
    x = -wlin
    softplus = jnp.maximum(x, 0.0) + jnp.log(1.0 + jnp.exp(-jnp.abs(x)))
    wlog = -softplus - 0.5
    logdecay = -jnp.exp(wlog)
    a = _sigmoid(a0_ref[...] + jnp.dot(lwla.astype(BF16), aup_ref[...], preferred_element_type=F32))
    g = jnp.dot(_sigmoid(lg).astype(BF16), gup_ref[...], preferred_element_type=F32)
    kk = k * kk_ref[...]
    ss = _dot_x2(kk * kk, ones_ref[...])
    kk = kk * lax.rsqrt(jnp.maximum(ss, 1e-24))
    k2 = k * (1.0 + (a - 1.0) * ka_ref[...])
    bonus_ref[0] = _dot_x2(r * k2 * rk_ref[...], ones_ref[...]) * v
    g_ref[0] = g

    cs = _dot_w3(tri_ref[...], logdecay)
    nchunk = tt // CHUNK
    cs3 = cs.reshape(nchunk, CHUNK, w)
    csl3 = cs3[:, CHUNK - 1:CHUNK, :]
    csl = jnp.broadcast_to(csl3, (nchunk, CHUNK, w)).reshape(tt, w)
    gl_ref[0] = jnp.broadcast_to(jnp.exp(csl3), (nchunk, SUBLANES, w))
    inv_g = jnp.exp(-cs)
    to_end = jnp.exp(csl - cs)
    avec = -kk
    bvec = kk * a
    at_ref[0] = avec * jnp.exp(cs - logdecay)
    bt_ref[0] = bvec * inv_g
    kt_ref[0] = k2 * inv_g
    rt_ref[0] = r * jnp.exp(cs)
    bh_ref[0] = bvec * to_end
    kh_ref[0] = k2 * to_end
    v_ref[0] = v


def _rwkv_prep(z, prev0, mu, w0, wup, a0, aup, gup, k_k, k_a, r_k, ones_bd, tri_bd, tt):
    b, t, _ = z.shape
    w = RWKV_WIDTH
    seq = pl.BlockSpec((1, tt, w), lambda i, j: (i, j, 0))
    seq_shape = jax.ShapeDtypeStruct((b, t, w), F32)
    consts = (prev0, mu, w0, wup, a0, aup, gup, k_k, k_a, r_k, ones_bd, tri_bd)
    return pl.pallas_call(
        functools.partial(_rwkv_prep_body, tt=tt),
        grid=(b, t // tt),
        in_specs=[pl.BlockSpec((1, tt, SHIFT_TOTAL), lambda i, j: (i, j, 0))] + [_full(c.shape) for c in consts],
        out_specs=[seq] * 7 + [pl.BlockSpec((1, tt // CHUNK, SUBLANES, w), lambda i, j: (i, j, 0, 0)), seq, seq],
        out_shape=[seq_shape] * 7 + [jax.ShapeDtypeStruct((b, t // CHUNK, SUBLANES, w), F32), seq_shape, seq_shape],
        scratch_shapes=[pltpu.VMEM((tt + SUBLANES, SHIFT_TOTAL), F32)],
        compiler_params=_params(("parallel", "arbitrary")),
    )(z, *consts)


def _chunk_body(at_ref, bt_ref, kt_ref, rt_ref, bh_ref, kh_ref, v_ref,
                rh_ref, o0_ref, p_ref, q_ref, *, cps):
    gw = GROUP_LANES
    heads = gw // RWKV_HEAD
    rows = heads * CHUNK
    ri = lax.broadcasted_iota(jnp.int32, (rows, rows), 0)
    ci = lax.broadcasted_iota(jnp.int32, (rows, rows), 1)
    same = (ri // CHUNK) == (ci // CHUNK)
    strict = same & (ci < ri)
    incl = same & (ci <= ri)
    eye = ri == ci
    lane = lax.broadcasted_iota(jnp.int32, (1, gw), 1) // RWKV_HEAD
    lri = lax.broadcasted_iota(jnp.int32, (gw, gw), 0)
    lci = lax.broadcasted_iota(jnp.int32, (gw, gw), 1)
    lsame = (lri // RWKV_HEAD) == (lci // RWKV_HEAD)

    def stack(x):
        return jnp.concatenate([jnp.where(lane == i, x, 0.0) for i in range(heads)], axis=0)

    def unstack(x):
        out = x[0:CHUNK]
        for i in range(1, heads):
            out = out + x[i * CHUNK:(i + 1) * CHUNK]
        return out

    chains = [(ck, grp) for ck in range(cps) for grp in range(RWKV_WIDTH // gw)]

    def load(ref):
        return [ref[0, ck * CHUNK:(ck + 1) * CHUNK, grp * gw:(grp + 1) * gw] for ck, grp in chains]

    def each(fn, *lists):
        return [fn(*args) for args in zip(*lists)]

    at, bt, kt, rt, bh, kh, v = (load(r) for r in (at_ref, bt_ref, kt_ref, rt_ref, bh_ref, kh_ref, v_ref))
    xa, xr, yb, yk, vs = (each(stack, x) for x in (at, rt, bt, kt, v))
    a_ab = each(lambda x, y: jnp.where(strict, _dot_nt(x, y), 0.0), xa, yb)
    a_ak = each(lambda x, y: jnp.where(strict, _dot_nt(x, y), 0.0), xa, yk)
    a_rb = each(lambda x, y: jnp.where(incl, _dot_nt(x, y), 0.0), xr, yb)
    a_rk = each(lambda x, y: jnp.where(incl, _dot_nt(x, y), 0.0), xr, yk)
    tm = each(lambda a: jnp.where(eye, 1.0, 0.0) + a, a_ab)
    apow = a_ab
    for _ in range(int(math.log2(CHUNK)) - 1):
        apow = each(lambda a: _dot(a, a), apow)
        tm = each(lambda t, a: t + _dot(t, a), tm, apow)
    akv = each(_dot, a_ak, vs)
    tw = each(lambda t, x, y: _dot(t, jnp.concatenate([x, y], axis=1)), tm, xa, akv)
    rb = each(_dot, a_rb, tw)
    ov = each(_dot, a_rk, vs)
    rhat = each(lambda x, r: unstack(x + r[:, :gw]), xr, rb)
    o0 = each(lambda o, r: unstack(o + r[:, gw:]), ov, rb)
    ahat = each(lambda t: unstack(t[:, :gw]), tw)
    u0 = each(lambda t: unstack(t[:, gw:]), tw)
    pm = each(lambda a, b: jnp.where(lsame, _dot_tn(a, b), 0.0), ahat, bh)
    qm = each(lambda u, b, vv, k: jnp.where(lsame, _dot_tn(u, b) + _dot_tn(vv, k), 0.0), u0, bh, v, kh)
    for i, (ck, grp) in enumerate(chains):
        sl = slice(grp * gw, (grp + 1) * gw)
        tr = slice(ck * CHUNK, (ck + 1) * CHUNK)
        rh_ref[0, tr, sl] = rhat[i]
        o0_ref[0, tr, sl] = o0[i]
        for pr in range(gw // PAIR_LANES):
            ps = slice(pr * PAIR_LANES, (pr + 1) * PAIR_LANES)
            p_ref[0, ck, grp * (gw // PAIR_LANES) + pr] = pm[i][ps, ps]
            q_ref[0, ck, grp * (gw // PAIR_LANES) + pr] = qm[i][ps, ps]


def _chunk_transforms(at, bt, kt, rt, bh, kh, v):
    b, t, w = at.shape
    nc = t // CHUNK
    cps = 4 if nc % 4 == 0 else 1
    npair = w // PAIR_LANES
    seq = pl.BlockSpec((1, cps * CHUNK, w), lambda i, j: (i, j, 0))
    mat = pl.BlockSpec((1, cps, npair, PAIR_LANES, PAIR_LANES), lambda i, j: (i, j, 0, 0, 0))
    return pl.pallas_call(
        functools.partial(_chunk_body, cps=cps),
        grid=(b, nc // cps),
        in_specs=[seq] * 7,
        out_specs=[seq, seq, mat, mat],
        out_shape=[
            jax.ShapeDtypeStruct((b, t, w), F32),
            jax.ShapeDtypeStruct((b, t, w), F32),
            jax.ShapeDtypeStruct((b, nc, npair, PAIR_LANES, PAIR_LANES), F32),
            jax.ShapeDtypeStruct((b, nc, npair, PAIR_LANES, PAIR_LANES), F32),
        ],
        compiler_params=_params(("parallel", "parallel")),
    )(at, bt, kt, rt, bh, kh, v)


def _scan_body(rh_ref, o0_ref, p_ref, q_ref, gl_ref, s0_ref, o_ref, sfin_ref, s_ref):
    c = pl.program_id(0)
    nb, npair = s_ref.shape[0], s_ref.shape[1]

    @pl.when(c == 0)
    def _():
        for bi in range(nb):
            s_ref[bi] = s0_ref[...]

    for bi in range(nb):
        for pr in range(npair):
            ps = slice(pr * PAIR_LANES, (pr + 1) * PAIR_LANES)
            s = s_ref[bi, pr]
            s_hi = s.astype(BF16)
            s_lo = (s - s_hi.astype(F32)).astype(BF16)
            o_ref[bi, :, ps] = _dot_nt(rh_ref[bi, :, ps], s_hi) + o0_ref[bi, :, ps]
            p = p_ref[bi, 0, pr].astype(BF16)
            s_ref[bi, pr] = (s * gl_ref[bi, 0, 0:1, ps]
                             + jnp.dot(s_hi, p, preferred_element_type=F32)
                             + jnp.dot(s_lo, p, preferred_element_type=F32)
                             + q_ref[bi, 0, pr])

    @pl.when(c == pl.num_programs(0) - 1)
    def _():
        sfin_ref[...] = s_ref[...]


def _state_scan(rh, o0, p, q, gl, s0):
    b, t, w = rh.shape
    nc = t // CHUNK
    npair = w // PAIR_LANES
    seq = pl.BlockSpec((b, CHUNK, w), lambda c: (0, c, 0))
    mat = pl.BlockSpec((b, 1, npair, PAIR_LANES, PAIR_LANES), lambda c: (0, c, 0, 0, 0))
    return pl.pallas_call(
        _scan_body,
        grid=(nc,),
        in_specs=[seq, seq, mat, mat, pl.BlockSpec((b, 1, SUBLANES, w), lambda c: (0, c, 0, 0)), _full(s0.shape)],
        out_specs=[seq, _full((b, npair, PAIR_LANES, PAIR_LANES))],
        out_shape=[
            jax.ShapeDtypeStruct((b, t, w), F32),
            jax.ShapeDtypeStruct((b, npair, PAIR_LANES, PAIR_LANES), F32),
        ],
        scratch_shapes=[pltpu.VMEM((b, npair, PAIR_LANES, PAIR_LANES), F32)],
        compiler_params=_params(("arbitrary",)),
    )(rh, o0, p, q, gl, s0)


def _merge_body(o_ref, bonus_ref, g_ref, gng_ref, gnb_ref, avg_ref, wro_ref, yag_ref, gateb_ref, wo_ref, x_ref,
                h_ref):
    o = o_ref[...]
    mean = _dot_x2(o, avg_ref[...])
    d = o - mean
    var = _dot_x2(d * d, avg_ref[...])
    on = d * lax.rsqrt(var + GN_EPS) * gng_ref[...] + gnb_ref[...]
    xo = (on + bonus_ref[...]) * g_ref[...]
    yb = jnp.dot(xo.astype(BF16), wro_ref[...], preferred_element_type=F32)
    m = yag_ref[...] + gateb_ref[...] * yb
    h_ref[...] = x_ref[...] + jnp.dot(m.astype(BF16), wo_ref[...], preferred_element_type=F32)


def _merge(o, bonus, g, gng, gnb, avg_bd, wro, yag, gates, wo, x2d, tm):
    n, d = x2d.shape
    w = RWKV_WIDTH
    row_w = pl.BlockSpec((tm, w), lambda i: (i, 0))
    row_d = pl.BlockSpec((tm, d), lambda i: (i, 0))
    return pl.pallas_call(
        _merge_body,
        grid=(n // tm,),
        in_specs=[row_w, row_w, row_w, _full(gng.shape), _full(gnb.shape), _full(avg_bd.shape), _full(wro.shape),
                  row_d, pl.BlockSpec((tm, d), lambda i: (i, 1)), _full(wo.shape), row_d],
        out_specs=row_d,
        out_shape=jax.ShapeDtypeStruct((n, d), F32),
        compiler_params=_params(("parallel",)),
    )(o, bonus, g, gng, gnb, avg_bd, wro, yag, gates, wo, x2d)


PEER_KEY_TILES = 2
N_RANK = PEER_TOPK + 1
N_RANK_PAD = -(-N_RANK // (2 * SUBLANES)) * (2 * SUBLANES)


def _candidate_pairs():
    return [(i, j) for i in range(N_RANK) for j in range(N_RANK) if (i + 1) * (j + 1) <= N_RANK]


N_CAND = len(_candidate_pairs())
N_CAND_PAD = -(-N_CAND // (2 * SUBLANES)) * (2 * SUBLANES)


def _peer_body(h_ref, gffn_ref, wqt_ref, keys_ref, sela_ref, selb_ref, u_ref, vt_ref, gfin_ref, out_ref,
               xnt_ref, qt_ref, th_ref, e1_ref, s2_ref, e2_ref, top_ref, act_ref, c_ref, yt_ref,
               *, tb, nch):
    s = pl.program_id(1)
    ncol = tb // LANES
    neg_inf = -1e30

    def top_values(x):
        vals = []
        cur = x
        for _ in range(N_RANK):
            m = jnp.max(cur, axis=0, keepdims=True)
            vals.append(m)
            cur = jnp.where(cur == m, neg_inf, cur)
        return vals

    def prep():
        xnt = _rmsnorm(h_ref[...], gffn_ref[...]).T.astype(BF16)
        xnt_ref[...] = xnt
        qt_ref[...] = jnp.dot(wqt_ref[...], xnt, preferred_element_type=F32).astype(BF16)
        yt_ref[...] = jnp.zeros_like(yt_ref)
        top_ref[...] = jnp.zeros_like(top_ref)

        def per_head(hh, carry):
            base = pl.multiple_of(hh * (2 * PEER_DHALF), 2 * PEER_DHALF)
            s1 = jnp.dot(keys_ref[hh, 0], qt_ref[pl.ds(base, PEER_DHALF), :], preferred_element_type=F32)
            s2 = jnp.dot(keys_ref[hh, 1], qt_ref[pl.ds(base + PEER_DHALF, PEER_DHALF), :],
                         preferred_element_type=F32)
            for lc in range(ncol):
                ls = slice(lc * LANES, (lc + 1) * LANES)
                s1c = s1[:, ls]
                s2c = s2[:, ls]
                va = top_values(s1c)
                vb = top_values(s2c)
                for i in range(N_RANK):
                    top_ref[0, i:i + 1, :] = va[i]
                    top_ref[1, i:i + 1, :] = vb[i]
                row = lax.broadcasted_iota(jnp.int32, (N_CAND_PAD, LANES), 0)
                asel = _dot_w3(sela_ref[...], top_ref[0])
                bsel = _dot_w3(selb_ref[...], top_ref[1])
                cand = jnp.where(row < N_CAND, asel + bsel, neg_inf)
                best = top_values(cand)
                tau = 0.5 * (best[PEER_TOPK - 1] + best[PEER_TOPK])
                keepc = (bsel >= tau - asel) & (row < N_CAND)
                zsum = jnp.sum(jnp.where(keepc, jnp.exp(asel - va[0]) * jnp.exp(bsel - vb[0]), 0.0),
                               axis=0, keepdims=True)
                grouped = (PEER_NKEYS // SUBLANES, SUBLANES, LANES)
                th_ref[hh, lc] = (tau - s1c).reshape(grouped)
                e1_ref[hh, lc] = jnp.exp(s1c - va[0]).reshape(grouped)
                s2_ref[hh, lc] = s2c
                e2_ref[hh, lc] = jnp.exp(s2c - vb[0]) * (0.5 / zsum)
            return carry

        lax.fori_loop(0, PEER_HEADS, per_head, 0)

    @pl.when(s == 0)
    def _():
        prep()

    et = SUBLANES * PEER_NKEYS
    for kt in range(PEER_KEY_TILES):
        a_part = jnp.dot(u_ref[kt * et:(kt + 1) * et, :], xnt_ref[...], preferred_element_type=F32)
        for cc in range(ncol):
            act_ref[cc, kt * et:(kt + 1) * et, :] = a_part[:, cc * LANES:(cc + 1) * LANES]

    def per_col(lc, carry):
        for kt in range(PEER_KEY_TILES):
            tile = s * PEER_KEY_TILES + kt
            for i1 in range(SUBLANES):
                r0 = (kt * SUBLANES + i1) * PEER_NKEYS
                rows = slice(r0, r0 + PEER_NKEYS)
                acc = jnp.zeros((PEER_NKEYS, LANES), F32)
                for hh in range(PEER_HEADS):
                    keep = s2_ref[hh, lc] >= th_ref[hh, lc, tile, i1:i1 + 1, :]
                    acc = jnp.where(keep, acc + e1_ref[hh, lc, tile, i1:i1 + 1, :] * e2_ref[hh, lc], acc)
                a = act_ref[lc, rows, :]
                gelu2 = a + a * lax.erf(a * math.sqrt(0.5))
                c_ref[lc, rows, :] = (acc * gelu2).astype(BF16)
        return carry

    lax.fori_loop(0, ncol, per_col, 0)
    y_part = None
    for kt in range(PEER_KEY_TILES):
        coef = jnp.concatenate([c_ref[cc, kt * et:(kt + 1) * et, :] for cc in range(ncol)], axis=1)
        p = jnp.dot(vt_ref[:, kt * et:(kt + 1) * et], coef, preferred_element_type=F32)
        y_part = p if y_part is None else y_part + p
    yt_ref[...] += y_part

    @pl.when(s == nch - 1)
    def _():
        hh = h_ref[...] + yt_ref[...].T
        out_ref[...] = _rmsnorm(hh, gfin_ref[...])


def _peer(h1, gffn, wqt, keys, sela, selb, u, vt, gfin, tb):
    n, d = h1.shape
    ne = u.shape[0]
    ec = PEER_KEY_TILES * SUBLANES * PEER_NKEYS
    nch = ne // ec
    assert ne % ec == 0
    nq = wqt.shape[0]
    ncol = tb // LANES
    key_tiles = (PEER_HEADS, ncol, PEER_NKEYS // SUBLANES, SUBLANES, LANES)
    return pl.pallas_call(
        functools.partial(_peer_body, tb=tb, nch=nch),
        grid=(n // tb, nch),
        in_specs=[
            pl.BlockSpec((tb, d), lambda i, s: (i, 0)),
            _full(gffn.shape),
            pl.BlockSpec(wqt.shape, lambda i, s: (0, 0), pipeline_mode=pl.Buffered(1)),
            _full(keys.shape), _full(sela.shape), _full(selb.shape),
            pl.BlockSpec((ec, d), lambda i, s: (s, 0)),
            pl.BlockSpec((d, ec), lambda i, s: (0, s)),
            _full(gfin.shape),
        ],
        out_specs=pl.BlockSpec((tb, d), lambda i, s: (i, 0)),
        out_shape=jax.ShapeDtypeStruct((n, d), F32),
        scratch_shapes=[
            pltpu.VMEM((d, tb), BF16),
            pltpu.VMEM((nq, tb), BF16),
            pltpu.VMEM(key_tiles, F32),
            pltpu.VMEM(key_tiles, F32),
            pltpu.VMEM((PEER_HEADS, ncol, PEER_NKEYS, LANES), F32),
            pltpu.VMEM((PEER_HEADS, ncol, PEER_NKEYS, LANES), F32),
            pltpu.VMEM((2, N_RANK_PAD, LANES), F32),
            pltpu.VMEM((ncol, ec, LANES), F32),
            pltpu.VMEM((ncol, ec, LANES), BF16),
            pltpu.VMEM((d, tb), F32),
        ],
        compiler_params=_params(("parallel", "arbitrary")),
    )(h1, gffn, wqt, keys, sela, selb, u, vt, gfin)


def _block_diag_const(width, block, value):
    idx = jnp.arange(width) // block
    return jnp.where(idx[:, None] == idx[None, :], value, 0.0).astype(BF16)


def _tile(n, pref):
    t = min(n, pref)
    assert n % t == 0, (n, pref)
    return t


def kernel(x, meta_tokens, g_mix, w_in, conv_w, conv_b, conv_ln_g, conv_ln_b, w_conv_out, mu_shift, w0, w_up, a0, a_up, g_up, k_k, k_a, r_k, gn_g, gn_b, w_rwkv_out, w_o, g_ffn, w_q, sub_keys, expert_u, expert_v, g_final):
    depth = g_mix.shape[0]
    assert depth == 1
    b, t, d = x.shape
    n = b * t
    n_meta = meta_tokens.shape[0]
    assert n_meta <= min(HALO, CHUNK) and n_meta % SUBLANES == 0
    w = RWKV_WIDTH
    row = lambda p: p.reshape(1, -1).astype(F32)

    w_in0 = w_in[0]
    wc = w_in0[:, :2 * CONV_CH].astype(BF16)
    wr = w_in0[:, 2 * CONV_CH:2 * CONV_CH + SHIFT_TOTAL].astype(BF16)
    wg = w_in0[:, 2 * CONV_CH + SHIFT_TOTAL:].astype(BF16)
    g_mix0 = row(g_mix[0])
    zeros_w = jnp.zeros((RANK_W, w), F32)
    wup_pad = jnp.concatenate([w_up[0], jnp.zeros((RANK_A, w), F32)], axis=0).astype(BF16)
    aup_pad = jnp.concatenate([zeros_w, a_up[0]], axis=0).astype(BF16)
    ones_bd = _block_diag_const(w, RWKV_HEAD, 1.0)
    avg_bd = _block_diag_const(w, RWKV_HEAD, 1.0 / RWKV_HEAD)
    cw = jnp.concatenate([conv_w[0], jnp.zeros((1, CONV_CH), F32)], axis=0)

    x2d = x.reshape(n, d)
    u, zr, gates = _inproj(x2d, g_mix0, wc, wr, wg, _tile(n, 512))
    u_m, zr_m, _ = _inproj(meta_tokens.astype(F32), g_mix0, wc, wr, wg, n_meta)

    halo0 = jnp.concatenate([jnp.zeros((HALO - n_meta, CONV_CH), F32), u_m], axis=0)
    tt = _tile(t, 512)
    yag = _conv_branch(u.reshape(b, t, CONV_CH), halo0, cw, row(conv_b[0]), row(conv_ln_g[0]), row(conv_ln_b[0]),
                       w_conv_out[0].astype(BF16), gates.reshape(b, t, 2 * d), tt)

    tri = jnp.tril(jnp.ones((CHUNK, CHUNK), F32))
    prep_consts = (row(mu_shift[0]), row(w0[0]), wup_pad, row(a0[0]), aup_pad, g_up[0].astype(BF16),
                   row(k_k[0]), row(k_a[0]), row(r_k[0]), ones_bd)

    def rwkv(z, prev0, s0, tt_):
        tri_bd = jnp.kron(jnp.eye(tt_ // CHUNK, dtype=F32), tri).astype(BF16)
        outs = _rwkv_prep(z, prev0, *prep_consts, tri_bd, tt_)
        at, bt, kt, rt, bh, kh, v, gl, g, bonus = outs
        rh, o0, p, q = _chunk_transforms(at, bt, kt, rt, bh, kh, v)
        o, sfin = _state_scan(rh, o0, p, q, gl, s0)
        return o, g, bonus, sfin

    z_meta = jnp.concatenate([jnp.zeros((CHUNK - n_meta, SHIFT_TOTAL), F32), zr_m], axis=0)[None]
    s_zero = jnp.zeros((w // PAIR_LANES, PAIR_LANES, PAIR_LANES), F32)
    _, _, _, s_meta = rwkv(z_meta, jnp.zeros((1, SHIFT_TOTAL), F32), s_zero, CHUNK)
    o, g, bonus, _ = rwkv(zr.reshape(b, t, SHIFT_TOTAL), zr_m[n_meta - 1:n_meta], s_meta[0], tt)

    h1 = _merge(o.reshape(n, w), bonus.reshape(n, w), g.reshape(n, w), row(gn_g[0]), row(gn_b[0]), avg_bd,
                w_rwkv_out[0].astype(BF16), yag.reshape(n, d), gates, w_o[0].astype(BF16), x2d, _tile(n, 512))

    pairs = _candidate_pairs()
    sela = jnp.zeros((N_CAND_PAD, N_RANK_PAD), F32).at[jnp.arange(N_CAND), jnp.array([p[0] for p in pairs])].set(1.0)
    selb = jnp.zeros((N_CAND_PAD, N_RANK_PAD), F32).at[jnp.arange(N_CAND), jnp.array([p[1] for p in pairs])].set(1.0)
    out = _peer(h1, row(g_ffn[0]), w_q[0].T.astype(BF16), sub_keys[0].astype(BF16),
                sela.astype(BF16), selb.astype(BF16), expert_u[0].astype(BF16), expert_v[0].T.astype(BF16),
                row(g_final), _tile(n, 512))
    return out.reshape(b, t, d)
```
